```python
import jax, jax.numpy as jnp
from jax import lax
import numpy as np

D_MODEL = 1024
BATCH = 1
SEQ = 16384
DEPTH = 1

D_MIX = D_MODEL
SB_HEADS = 8
SB_HEAD_DIM = 64
D_SB = SB_HEADS * SB_HEAD_DIM
D_RG = D_MIX - D_SB
RG_BLOCKS = 8
RG_BLOCK_DIM = D_RG // RG_BLOCKS
CONV_WIDTH = 4
RG_C = 8.0
D_IN_PROJ = 3 * D_SB + 2 * D_RG
Q_BLOCK = 128
PEER_HEADS = 8
PEER_N_KEYS = 128
PEER_N_EXPERTS = PEER_N_KEYS * PEER_N_KEYS
PEER_D_KEY = 256
PEER_HALF = PEER_D_KEY // 2
PEER_TOPK = 16
TOKEN_CHUNK = 128
EPS = 1e-6

kernel_name = "hymba_stickbreak_rglru_peer"


def rms_norm(x, g):
    xf = x.astype(jnp.float32)
    y = xf * lax.rsqrt(jnp.mean(xf * xf, axis=-1, keepdims=True) + EPS)
    return (y * g.astype(jnp.float32)).astype(x.dtype)


def stick_breaking_attention(q, k, v):
    B, H, S, Dh = q.shape
    nb = S // Q_BLOCK
    scale = Dh ** -0.5
    kpos = jnp.arange(S)

    def block(i):
        q_blk = lax.dynamic_slice_in_dim(q, i * Q_BLOCK, Q_BLOCK, axis=2)
        qpos = i * Q_BLOCK + jnp.arange(Q_BLOCK)
        causal = kpos[None, :] < qpos[:, None]
        z = jnp.einsum('bhqd,bhkd->bhqk', q_blk, k).astype(jnp.float32) * scale
        log_beta = jax.nn.log_sigmoid(z)
        log_one_minus = jnp.where(causal, jax.nn.log_sigmoid(-z), 0.0)
        suffix = lax.cumsum(log_one_minus, axis=3, reverse=True) - log_one_minus
        w = jnp.where(causal, jnp.exp(log_beta + suffix), 0.0)
        return jnp.einsum('bhqk,bhkd->bhqd', w.astype(v.dtype), v)

    out = lax.map(block, jnp.arange(nb))
    return out.transpose(1, 2, 0, 3, 4).reshape(B, H, S, Dh)


def causal_depthwise_conv(x, w, bias):
    S = x.shape[1]
    xp = jnp.pad(x, ((0, 0), (CONV_WIDTH - 1, 0), (0, 0)))
    y = xp[:, 0:S, :] * w[0]
    for j in range(1, CONV_WIDTH):
        y = y + xp[:, j:j + S, :] * w[j]
    return y + bias


def rg_lru(x, w_a, b_a, w_x, b_x, lam):
    B, S, C = x.shape
    xb = x.reshape(B, S, RG_BLOCKS, RG_BLOCK_DIM)
    r = jax.nn.sigmoid(jnp.einsum('bsni,nij->bsnj', xb, w_a).reshape(B, S, C) + b_a)
    i_gate = jax.nn.sigmoid(jnp.einsum('bsni,nij->bsnj', xb, w_x).reshape(B, S, C) + b_x)
    log_a = -RG_C * r.astype(jnp.float32) * jax.nn.softplus(-lam.astype(jnp.float32))
    a = jnp.exp(log_a)
    b = jnp.sqrt(-jnp.expm1(2.0 * log_a)) * (i_gate * x).astype(jnp.float32)

    def step(h, ab):
        a_t, b_t = ab
        h = a_t * h + b_t
        return h, h

    h0 = jnp.zeros((B, C), jnp.float32)
    _, hs = lax.scan(step, h0, (a.transpose(1, 0, 2), b.transpose(1, 0, 2)))
    return hs.transpose(1, 0, 2).astype(x.dtype)


def peer_ffn(x, w_query, sub_keys, expert_u, expert_v):
    B, S, D = x.shape
    T = B * S
    xt = x.reshape(T // TOKEN_CHUNK, TOKEN_CHUNK, D)

    def chunk(xc):
        q = jnp.einsum('td,dhk->thk', xc, w_query)
        q = q.reshape(TOKEN_CHUNK, PEER_HEADS, 2, PEER_HALF)
        s = jnp.einsum('thpk,hpnk->thpn', q, sub_keys).astype(jnp.float32)
        top_s, top_i = lax.top_k(s, PEER_TOPK)
        cand_s = top_s[:, :, 0, :, None] + top_s[:, :, 1, None, :]
        cand_i = top_i[:, :, 0, :, None] * PEER_N_KEYS + top_i[:, :, 1, None, :]
        cand_s = cand_s.reshape(TOKEN_CHUNK, PEER_HEADS, PEER_TOPK * PEER_TOPK)
        cand_i = cand_i.reshape(TOKEN_CHUNK, PEER_HEADS, PEER_TOPK * PEER_TOPK)
        best_s, best_pos = lax.top_k(cand_s, PEER_TOPK)
        idx = jnp.take_along_axis(cand_i, best_pos, axis=-1)
        g = jax.nn.softmax(best_s, axis=-1)
        u = jnp.take(expert_u, idx, axis=0)
        v = jnp.take(expert_v, idx, axis=0)
        act = jax.nn.gelu(jnp.einsum('td,thkd->thk', xc, u))
        return jnp.einsum('thk,thkd->td', (g * act).astype(v.dtype), v)

    y = lax.map(chunk, xt)
    return y.reshape(B, S, D)


def setup_inputs(seed: int = 0) -> dict:
    key = jax.random.key(seed)
    ks = jax.random.split(key, 24)
    f32 = jnp.float32

    def nrm(k, shape, scale):
        return jax.random.normal(k, shape, f32) * scale

    def gain(k, shape):
        return 1.0 + 0.02 * jax.random.normal(k, shape, f32)

    u_a = jax.random.uniform(ks[11], (DEPTH, D_RG), f32, 0.9, 0.999)
    a_base = u_a ** (1.0 / RG_C)
    rg_lambda = jnp.log(a_base) - jnp.log1p(-a_base)
    return {
        "x": jax.random.normal(ks[0], (BATCH, SEQ, D_MODEL), f32),
        "norm_mix": gain(ks[1], (DEPTH, D_MODEL)),
        "w_in": nrm(ks[2], (DEPTH, D_MODEL, D_IN_PROJ), D_MODEL ** -0.5),
        "q_norm": gain(ks[3], (DEPTH, SB_HEAD_DIM)),
        "k_norm": gain(ks[4], (DEPTH, SB_HEAD_DIM)),
        "conv_w": nrm(ks[5], (DEPTH, CONV_WIDTH, D_RG), CONV_WIDTH ** -0.5),
        "conv_b": nrm(ks[6], (DEPTH, D_RG), 0.01),
        "rg_w_a": nrm(ks[7], (DEPTH, RG_BLOCKS, RG_BLOCK_DIM, RG_BLOCK_DIM), RG_BLOCK_DIM ** -0.5),
        "rg_b_a": nrm(ks[8], (DEPTH, D_RG), 0.01),
        "rg_w_x": nrm(ks[9], (DEPTH, RG_BLOCKS, RG_BLOCK_DIM, RG_BLOCK_DIM), RG_BLOCK_DIM ** -0.5),
        "rg_b_x": nrm(ks[10], (DEPTH, D_RG), 0.01),
        "rg_lambda": rg_lambda,
        "out_norm_sb": gain(ks[12], (DEPTH, D_SB)),
        "out_norm_rg": gain(ks[13], (DEPTH, D_RG)),
        "w_out": nrm(ks[14], (DEPTH, D_MIX, D_MODEL), D_MIX ** -0.5),
        "norm_ffn": gain(ks[15], (DEPTH, D_MODEL)),
        "peer_w_query": nrm(ks[16], (DEPTH, D_MODEL, PEER_HEADS, PEER_D_KEY), D_MODEL ** -0.5),
        "peer_sub_keys": nrm(ks[17], (DEPTH, PEER_HEADS, 2, PEER_N_KEYS, PEER_HALF), PEER_HALF ** -0.5),
        "peer_u": nrm(ks[18], (DEPTH, PEER_N_EXPERTS, D_MODEL), D_MODEL ** -0.5),
        "peer_v": nrm(ks[19], (DEPTH, PEER_N_EXPERTS, D_MODEL), PEER_HEADS ** -0.5),
    }


def reference(x, norm_mix, w_in, q_norm, k_norm, conv_w, conv_b, rg_w_a, rg_b_a,
              rg_w_x, rg_b_x, rg_lambda, out_norm_sb, out_norm_rg, w_out, norm_ffn,
              peer_w_query, peer_sub_keys, peer_u, peer_v):
    B, S, _ = x.shape
    for l in range(DEPTH):
        h = rms_norm(x, norm_mix[l])
        proj = jnp.einsum('bsd,de->bse', h, w_in[l])
        q = proj[:, :, 0:D_SB]
        k = proj[:, :, D_SB:2 * D_SB]
        v = proj[:, :, 2 * D_SB:3 * D_SB]
        x_rg = proj[:, :, 3 * D_SB:3 * D_SB + D_RG]
        g_rg = proj[:, :, 3 * D_SB + D_RG:D_IN_PROJ]

        def to_heads(t):
            return t.reshape(B, S, SB_HEADS, SB_HEAD_DIM).transpose(0, 2, 1, 3)

        qh = rms_norm(to_heads(q), q_norm[l])
        kh = rms_norm(to_heads(k), k_norm[l])
        o_sb = stick_breaking_attention(qh, kh, to_heads(v))
        o_sb = o_sb.transpose(0, 2, 1, 3).reshape(B, S, D_SB)

        x_rg = causal_depthwise_conv(x_rg, conv_w[l], conv_b[l])
        o_rg = rg_lru(x_rg, rg_w_a[l], rg_b_a[l], rg_w_x[l], rg_b_x[l], rg_lambda[l])
        o_rg = o_rg * jax.nn.gelu(g_rg)

        mixed = jnp.concatenate(
            [rms_norm(o_sb, out_norm_sb[l]), rms_norm(o_rg, out_norm_rg[l])], axis=-1)
        x = x + jnp.einsum('bse,ed->bsd', mixed, w_out[l])

        h = rms_norm(x, norm_ffn[l])
        x = x + peer_ffn(h, peer_w_query[l], peer_sub_keys[l], peer_u[l], peer_v[l])
    return x
```

```python
import functools

import jax
import jax.numpy as jnp
from jax import lax
from jax.experimental import pallas as pl
from jax.experimental.pallas import tpu as pltpu

F32 = jnp.float32
BF16 = jnp.bfloat16

D_MODEL = 1024
SB_HEADS = 8
SB_HEAD_DIM = 64
D_SB = SB_HEADS * SB_HEAD_DIM
D_RG = D_MODEL - D_SB
RG_BLOCKS = 8
RG_BLOCK_DIM = D_RG // RG_BLOCKS
CONV_WIDTH = 4
RG_C = 8.0
PEER_HEADS = 8
PEER_N_KEYS = 128
PEER_D_KEY = 256
PEER_HALF = PEER_D_KEY // 2
PEER_TOPK = 16
EPS = 1e-6

VMEM_LIMIT_BYTES = 56 * 1024 * 1024
SUBLANES = 8
LANES = 128

INPROJ_TOKENS = 512
ATT_Q = 512
ATT_K = 256
MIX_TOKENS = 256
PEER_TOKENS = 512
PEER_EXPERTS = 1024

NEG_INF = float("-inf")


def _nt_dot(a, b):
    return lax.dot_general(a, b, (((1,), (1,)), ((), ())), preferred_element_type=F32)


def _softplus(z):
    return jnp.maximum(z, 0.0) + jnp.log(1.0 + jnp.exp(-jnp.abs(z)))


def _gelu_tanh(x):
    c0 = 0.7978845608028654
    c1 = 0.044715 * c0
    half = 0.5 * x
    return half + half * jnp.tanh(x * (c1 * (x * x) + c0))


def _inproj_kernel(x_ref, gmix_ref, wnat_ref, wtq_ref, wtv_ref, qg_ref, kg_ref, bd_ref,
                   qt_ref, k_ref, vt_ref, xrg_ref, grg_ref):
    x = x_ref[...]
    ms = jnp.mean(x * x, axis=-1, keepdims=True)
    h = (x * lax.rsqrt(ms + EPS) * gmix_ref[...]).astype(BF16)

    nat = jnp.dot(h, wnat_ref[...], preferred_element_type=F32)
    k = nat[:, :D_SB]
    kk = k * k
    kk_hi = kk.astype(BF16)
    kk_lo = (kk - kk_hi.astype(F32)).astype(BF16)
    kss = (jnp.dot(kk_hi, bd_ref[...], preferred_element_type=F32)
           + jnp.dot(kk_lo, bd_ref[...], preferred_element_type=F32))
    k_ref[...] = (k * lax.rsqrt(kss * (1.0 / SB_HEAD_DIM) + EPS) * kg_ref[...]).astype(BF16)
    xrg_ref[...] = nat[:, D_SB:D_SB + D_RG]
    grg_ref[...] = nat[:, D_SB + D_RG:]

    t = x.shape[0]
    qt = _nt_dot(wtq_ref[...], h).reshape(SB_HEADS, SB_HEAD_DIM, t)
    qss = jnp.sum(qt * qt, axis=1, keepdims=True)
    qn = qt * lax.rsqrt(qss * (1.0 / SB_HEAD_DIM) + EPS)
    qn = qn.reshape(D_SB, t) * (qg_ref[...] * (SB_HEAD_DIM ** -0.5))
    qt_ref[...] = qn.astype(BF16)

    vt = _nt_dot(wtv_ref[...], h).astype(BF16)
    for c in range(t // ATT_K):
        vt_ref[c] = vt[:, c * ATT_K:(c + 1) * ATT_K]


def _inproj(x2d, gmix, wnat, wtq, wtv, qg, kg, bd):
    s = x2d.shape[0]
    t = INPROJ_TOKENS
    n_nat = wnat.shape[1]
    const = lambda shape: pl.BlockSpec(shape, lambda i: (0,) * len(shape))
    return pl.pallas_call(
        _inproj_kernel,
        grid=(s // t,),
        in_specs=[
            pl.BlockSpec((t, D_MODEL), lambda i: (i, 0)),
            const((1, D_MODEL)),
            const((D_MODEL, n_nat)),
            const((D_SB, D_MODEL)),
            const((D_SB, D_MODEL)),
            const((D_SB, 1)),
            const((1, D_SB)),
            const((D_SB, D_SB)),
        ],
        out_specs=[
            pl.BlockSpec((D_SB, t), lambda i: (0, i)),
            pl.BlockSpec((t, D_SB), lambda i: (i, 0)),
            pl.BlockSpec((t // ATT_K, D_SB, ATT_K), lambda i: (i, 0, 0)),
            pl.BlockSpec((t, D_RG), lambda i: (i, 0)),
            pl.BlockSpec((t, D_RG), lambda i: (i, 0)),
        ],
        out_shape=[
            jax.ShapeDtypeStruct((D_SB, s), BF16),
            jax.ShapeDtypeStruct((s, D_SB), BF16),
            jax.ShapeDtypeStruct((s // ATT_K, D_SB, ATT_K), BF16),
            jax.ShapeDtypeStruct((s, D_RG), F32),
            jax.ShapeDtypeStruct((s, D_RG), F32),
        ],
        compiler_params=pltpu.CompilerParams(
            dimension_semantics=("arbitrary",), vmem_limit_bytes=VMEM_LIMIT_BYTES),
        name="inproj",
    )(x2d, gmix, wnat, wtq, wtv, qg, kg, bd)


def _attn_kernel(qt_ref, k_ref, vt_ref, tri_ref, ot_ref):
    hd = pl.program_id(0)
    j = pl.program_id(1)
    q = qt_ref[...]
    zero = jnp.zeros_like(q)
    qpad = jnp.where(hd % 2 == 0,
                     jnp.concatenate([q, zero], axis=0),
                     jnp.concatenate([zero, q], axis=0))
    tri = tri_ref[...]
    subs = ATT_Q // ATT_K

    def tile(kt, carry, acc, masked, q0):
        ks = pl.multiple_of(kt * ATT_K, ATT_K)
        z = jnp.dot(k_ref[pl.ds(ks, ATT_K), :], qpad, preferred_element_type=F32)
        sp = _softplus(z)
        if masked:
            kpos = ks + lax.broadcasted_iota(jnp.int32, z.shape, 0)
            qpos = q0 + lax.broadcasted_iota(jnp.int32, z.shape, 1)
            valid = kpos < qpos
            sp = jnp.where(valid, sp, 0.0)
        spb = sp.astype(BF16)
        suf = jnp.dot(tri, spb, preferred_element_type=F32)
        w = jnp.exp(z - sp - suf - carry)
        if masked:
            w = jnp.where(valid, w, 0.0)
        acc = acc + jnp.dot(vt_ref[kt], w.astype(BF16), preferred_element_type=F32)
        carry = carry + (suf[0:1, :] + spb[0:1, :].astype(F32))
        return carry, acc

    carry = jnp.zeros((1, ATT_Q), F32)
    acc = jnp.zeros((SB_HEAD_DIM, ATT_Q), F32)
    q0 = j * ATT_Q
    for c in range(subs - 1, -1, -1):
        carry, acc = tile(j * subs + c, carry, acc, True, q0)

    def body(i, ca):
        carry, acc = ca
        sb = j - 1 - i
        for c in range(subs - 1, -1, -1):
            carry, acc = tile(sb * subs + c, carry, acc, False, q0)
        return carry, acc

    carry, acc = lax.fori_loop(0, j, body, (carry, acc))
    ot_ref[...] = acc


def _attention(qt, k, vt, tri):
    s = k.shape[0]
    return pl.pallas_call(
        _attn_kernel,
        grid=(SB_HEADS, s // ATT_Q),
        in_specs=[
            pl.BlockSpec((SB_HEAD_DIM, ATT_Q), lambda h, j: (h, j)),
            pl.BlockSpec((s, 2 * SB_HEAD_DIM), lambda h, j: (0, h // 2)),
            pl.BlockSpec((s // ATT_K, SB_HEAD_DIM, ATT_K), lambda h, j: (0, h, 0)),
            pl.BlockSpec((ATT_K, ATT_K), lambda h, j: (0, 0)),
        ],
        out_specs=pl.BlockSpec((SB_HEAD_DIM, ATT_Q), lambda h, j: (h, j)),
        out_shape=jax.ShapeDtypeStruct((D_SB, s), F32),
        compiler_params=pltpu.CompilerParams(
            dimension_semantics=("arbitrary", "arbitrary"), vmem_limit_bytes=VMEM_LIMIT_BYTES),
        name="sb_attention",
    )(qt, k, vt, tri)


def _mix_kernel(xrg_ref, grg_ref, ot_ref, x_ref, cw_ref, cb_ref, wa_ref, ba_ref, wx_ref, bx_ref,
                lam_ref, nsb_ref, nrg_ref, wout_ref, nffn_ref,
                x2_ref, h2_ref, xs_ref, hs_ref, hstate_ref):
    t = xrg_ref.shape[0]
    hist = SUBLANES

    @pl.when(pl.program_id(0) == 0)
    def _():
        xs_ref[0:hist, :] = jnp.zeros((hist, D_RG), F32)
        hstate_ref[...] = jnp.zeros_like(hstate_ref)

    xs_ref[hist:hist + t, :] = xrg_ref[...]
    y = cb_ref[...] + cw_ref[0:1, :] * xs_ref[hist - 3:hist - 3 + t, :]
    for jj in range(1, CONV_WIDTH):
        y = y + cw_ref[jj:jj + 1, :] * xs_ref[hist - 3 + jj:hist - 3 + jj + t, :]
    xs_ref[0:hist, :] = xs_ref[t:t + hist, :]

    yb = y.astype(BF16)
    r = jax.nn.sigmoid(jnp.dot(yb, wa_ref[...], preferred_element_type=F32) + ba_ref[...])
    ig = jax.nn.sigmoid(jnp.dot(yb, wx_ref[...], preferred_element_type=F32) + bx_ref[...])
    log_a = (-RG_C) * r * _softplus(-lam_ref[...])
    a = jnp.exp(log_a)
    b = jnp.sqrt(jnp.tanh(-log_a) * (1.0 + a * a)) * (ig * y)

    rowmod = lax.broadcasted_iota(jnp.int32, (t, D_RG), 0) % SUBLANES
    d = 1
    while d < SUBLANES:
        keep = rowmod >= d
        a_sh = pltpu.roll(a, d, axis=0)
        b_sh = pltpu.roll(b, d, axis=0)
        b = jnp.where(keep, b + a * b_sh, b)
        a = jnp.where(keep, a * a_sh, a)
        d *= 2
    hprev = hstate_ref[...]
    for g in range(t // SUBLANES):
        sl = slice(g * SUBLANES, (g + 1) * SUBLANES)
        hg = a[sl] * hprev + b[sl]
        hs_ref[sl, :] = hg
        hprev = jnp.broadcast_to(hg[SUBLANES - 1:SUBLANES, :], (SUBLANES, D_RG))
    hstate_ref[...] = hprev

    o_rg = hs_ref[...] * _gelu_tanh(grg_ref[...])
    o_sb = ot_ref[...].T

    def rms(v, g):
        return v * lax.rsqrt(jnp.mean(v * v, axis=-1, keepdims=True) + EPS) * g

    n_sb = rms(o_sb, nsb_ref[...]).astype(BF16)
    n_rg = rms(o_rg, nrg_ref[...]).astype(BF16)
    mix = (jnp.dot(n_sb, wout_ref[0:D_SB, :], preferred_element_type=F32)
           + jnp.dot(n_rg, wout_ref[D_SB:, :], preferred_element_type=F32))
    x2 = x_ref[...] + mix
    x2_ref[...] = x2
    h2_ref[...] = rms(x2, nffn_ref[...]).astype(BF16)


def _mix(xrg, grg, ot, x2d, cw, cb, wa, ba, wx, bx, lam, nsb, nrg, wout, nffn):
    s = x2d.shape[0]
    t = MIX_TOKENS
    const = lambda shape: pl.BlockSpec(shape, lambda i: (0,) * len(shape))
    return pl.pallas_call(
        _mix_kernel,
        grid=(s // t,),
        in_specs=[
            pl.BlockSpec((t, D_RG), lambda i: (i, 0)),
            pl.BlockSpec((t, D_RG), lambda i: (i, 0)),
            pl.BlockSpec((D_SB, t), lambda i: (0, i)),
            pl.BlockSpec((t, D_MODEL), lambda i: (i, 0)),
            const((CONV_WIDTH, D_RG)), const((1, D_RG)),
            const((D_RG, D_RG)), const((1, D_RG)),
            const((D_RG, D_RG)), const((1, D_RG)),
            const((1, D_RG)), const((1, D_SB)), const((1, D_RG)),
            const((D_MODEL, D_MODEL)), const((1, D_MODEL)),
        ],
        out_specs=[
            pl.BlockSpec((t, D_MODEL), lambda i: (i, 0)),
            pl.BlockSpec((t, D_MODEL), lambda i: (i, 0)),
        ],
        out_shape=[
            jax.ShapeDtypeStruct((s, D_MODEL), F32),
            jax.ShapeDtypeStruct((s, D_MODEL), BF16),
        ],
        scratch_shapes=[
            pltpu.VMEM((t + SUBLANES, D_RG), F32),
            pltpu.VMEM((t, D_RG), F32),
            pltpu.VMEM((SUBLANES, D_RG), F32),
        ],
        compiler_params=pltpu.CompilerParams(
            dimension_semantics=("arbitrary",), vmem_limit_bytes=VMEM_LIMIT_BYTES),
        name="rglru_outproj",
    )(xrg, grg, ot, x2d, cw, cb, wa, ba, wx, bx, lam, nsb, nrg, wout, nffn)


_N_RANK = PEER_TOPK + 1
_CAND_PAIRS = [(a, b) for a in range(_N_RANK) for b in range(_N_RANK)
               if (a + 1) * (b + 1) <= _N_RANK]


def _top_desc(s, n):
    out = []
    for _ in range(n):
        m = jnp.max(s, axis=0, keepdims=True)
        out.append(m)
        s = jnp.where(s == m, NEG_INF, s)
    return out


def _peer_route(h, wq_ref, keys_ref, e1_ref, th_ref, e2_ref, s2_ref):
    for hd in range(PEER_HEADS):
        qt = _nt_dot(wq_ref[hd * PEER_D_KEY:(hd + 1) * PEER_D_KEY, :], h).astype(BF16)
        s1 = jnp.dot(keys_ref[hd, 0], qt[:PEER_HALF], preferred_element_type=F32)
        s2 = jnp.dot(keys_ref[hd, 1], qt[PEER_HALF:], preferred_element_type=F32)
        top1 = _top_desc(s1, _N_RANK)
        top2 = _top_desc(s2, _N_RANK)
        cand = jnp.concatenate([top1[a] + top2[b] for a, b in _CAND_PAIRS], axis=0)
        best = _top_desc(cand, _N_RANK)
        zsum = jnp.ones_like(best[0])
        for kk in range(1, PEER_TOPK):
            zsum = zsum + jnp.exp(best[kk] - best[0])
        tau = 0.5 * (best[PEER_TOPK - 1] + best[PEER_TOPK])
        e1_ref[hd] = jnp.exp(s1 - top1[0]) / zsum
        th_ref[hd] = tau - s1
        e2_ref[hd] = jnp.exp(s2 - top2[0])
        s2_ref[hd] = s2


def _peer_kernel(h_ref, x2_ref, wq_ref, keys_ref, u_ref, vt_ref, out_ref,
                 e1_ref, th_ref, e2_ref, s2_ref, p_ref, acc_ref):
    e = pl.program_id(1)
    t = h_ref.shape[0]
    n1 = PEER_EXPERTS // PEER_N_KEYS
    assert n1 == SUBLANES

    @pl.when(e == 0)
    def _():
        _peer_route(h_ref[...], wq_ref, keys_ref, e1_ref, th_ref, e2_ref, s2_ref)
        acc_ref[...] = jnp.zeros_like(acc_ref)

    act = _gelu_tanh(_nt_dot(u_ref[...], h_ref[...]))
    i1_base = pl.multiple_of(e * n1, SUBLANES)
    for il in range(n1):
        for tc in range(t // LANES):
            ls = slice(tc * LANES, (tc + 1) * LANES)
            w = jnp.zeros((PEER_N_KEYS, LANES), F32)
            for hd in range(PEER_HEADS):
                th = th_ref[hd, pl.ds(i1_base, n1), ls][il:il + 1]
                e1 = e1_ref[hd, pl.ds(i1_base, n1), ls][il:il + 1]
                w = w + jnp.where(s2_ref[hd, :, ls] > th, e1 * e2_ref[hd, :, ls], 0.0)
            rs = slice(il * PEER_N_KEYS, (il + 1) * PEER_N_KEYS)
            p_ref[rs, ls] = (w * act[rs, ls]).astype(BF16)
    acc_ref[...] += jnp.dot(vt_ref[...], p_ref[...], preferred_element_type=F32)

    @pl.when(e == pl.num_programs(1) - 1)
    def _():
        out_ref[...] = x2_ref[...] + acc_ref[...].T


def _peer(h2, x2, wq_t, keys, u_bf, vt_bf):
    s = h2.shape[0]
    t = PEER_TOKENS
    n_exp = u_bf.shape[0]
    et = PEER_EXPERTS
    rt = lambda: pltpu.VMEM((PEER_HEADS, PEER_N_KEYS, t), F32)
    return pl.pallas_call(
        _peer_kernel,
        grid=(s // t, n_exp // et),
        in_specs=[
            pl.BlockSpec((t, D_MODEL), lambda i, e: (i, 0)),
            pl.BlockSpec((t, D_MODEL), lambda i, e: (i, 0)),
            pl.BlockSpec((PEER_HEADS * PEER_D_KEY, D_MODEL), lambda i, e: (0, 0)),
            pl.BlockSpec((PEER_HEADS, 2, PEER_N_KEYS, PEER_HALF), lambda i, e: (0, 0, 0, 0)),
            pl.BlockSpec((et, D_MODEL), lambda i, e: (e, 0)),
            pl.BlockSpec((D_MODEL, et), lambda i, e: (0, e)),
        ],
        out_specs=pl.BlockSpec((t, D_MODEL), lambda i, e: (i, 0)),
        out_shape=jax.ShapeDtypeStruct((s, D_MODEL), F32),
        scratch_shapes=[rt(), rt(), rt(), rt(),
                        pltpu.VMEM((et, t), BF16),
                        pltpu.VMEM((D_MODEL, t), F32)],
        compiler_params=pltpu.CompilerParams(
            dimension_semantics=("arbitrary", "arbitrary"), vmem_limit_bytes=VMEM_LIMIT_BYTES),
        name="peer",
    )(h2, x2, wq_t, keys, u_bf, vt_bf)


def _block_diag(w):
    n, bi, bj = w.shape
    eye = jnp.eye(n, dtype=w.dtype)
    return (eye[:, None, :, None] * w[:, :, None, :]).reshape(n * bi, n * bj)


def kernel(x, norm_mix, w_in, q_norm, k_norm, conv_w, conv_b, rg_w_a, rg_b_a, rg_w_x, rg_b_x,
           rg_lambda, out_norm_sb, out_norm_rg, w_out, norm_ffn, peer_w_query, peer_sub_keys,
           peer_u, peer_v):
    bsz, s, d = x.shape
    assert bsz == 1 and d == D_MODEL
    assert s % max(INPROJ_TOKENS, ATT_Q, MIX_TOKENS, PEER_TOKENS) == 0
    depth = w_in.shape[0]
    x2d = x.reshape(s, d)

    head_id = jnp.arange(D_SB) // SB_HEAD_DIM
    bd = (head_id[:, None] == head_id[None, :]).astype(BF16)
    kidx = jnp.arange(ATT_K)
    tri = (kidx[None, :] > kidx[:, None]).astype(BF16)
    row = lambda v: v.reshape(1, -1)

    for l in range(depth):
        w = w_in[l]
        wtq = w[:, 0:D_SB].T.astype(BF16)
        wtv = w[:, 2 * D_SB:3 * D_SB].T.astype(BF16)
        wnat = jnp.concatenate([w[:, D_SB:2 * D_SB], w[:, 3 * D_SB:]], axis=1).astype(BF16)
        qg = jnp.tile(q_norm[l], SB_HEADS).reshape(D_SB, 1)
        kg = jnp.tile(k_norm[l], SB_HEADS).reshape(1, D_SB)
        qt, k, vt, xrg, grg = _inproj(x2d, row(norm_mix[l]), wnat, wtq, wtv, qg, kg, bd)

        ot = _attention(qt, k, vt, tri)

        x2, h2 = _mix(xrg, grg, ot, x2d, conv_w[l], row(conv_b[l]),
                      _block_diag(rg_w_a[l]).astype(BF16), row(rg_b_a[l]),
                      _block_diag(rg_w_x[l]).astype(BF16), row(rg_b_x[l]),
                      row(rg_lambda[l]), row(out_norm_sb[l]), row(out_norm_rg[l]),
                      w_out[l].astype(BF16), row(norm_ffn[l]))

        wq_t = peer_w_query[l].reshape(d, PEER_HEADS * PEER_D_KEY).T.astype(BF16)
        x2d = _peer(h2, x2, wq_t, peer_sub_keys[l].astype(BF16),
                    peer_u[l].astype(BF16), peer_v[l].T.astype(BF16))
    return x2d.reshape(bsz, s, d)
```

```python
import functools

import jax
import jax.numpy as jnp
from jax import lax
from jax.experimental import pallas as pl
from jax.experimental.pallas import tpu as pltpu

F32 = jnp.float32
BF16 = jnp.bfloat16

D_MODEL = 1024
SB_HEADS = 8
SB_HEAD_DIM = 64
D_SB = SB_HEADS * SB_HEAD_DIM
D_RG = D_MODEL - D_SB
RG_BLOCKS = 8
RG_BLOCK_DIM = D_RG // RG_BLOCKS
CONV_WIDTH = 4
RG_C = 8.0
PEER_HEADS = 8
PEER_N_KEYS = 128
PEER_D_KEY = 256
PEER_HALF = PEER_D_KEY // 2
PEER_TOPK = 16
EPS = 1e-6

VMEM_LIMIT_BYTES = 56 * 1024 * 1024
SUBLANES = 8
LANES = 128

INPROJ_TOKENS = 512
ATT_Q = 512
ATT_K = 256
MIX_TOKENS = 256
PEER_TOKENS = 512
PEER_EXPERTS = 1024
PEER_CHUNK = 256

NEG_INF = float("-inf")
LOG2_E = 1.4426950408889634


def _nt_dot(a, b):
    return lax.dot_general(a, b, (((1,), (1,)), ((), ())), preferred_element_type=F32)


def _softplus(z):
    return jnp.maximum(z, 0.0) + jnp.log(1.0 + jnp.exp(-jnp.abs(z)))


def _gelu_tanh(x):
    c0 = 0.7978845608028654
    c1 = 0.044715 * c0
    half = 0.5 * x
    return half + half * jnp.tanh(x * (c1 * (x * x) + c0))


def _inproj_kernel(x_ref, gmix_ref, wnat_ref, wtq_ref, wtv_ref, qg_ref, kg_ref, bd_ref,
                   qt_ref, k_ref, vt_ref, xrg_ref, grg_ref):
    x = x_ref[...]
    ms = jnp.mean(x * x, axis=-1, keepdims=True)
    h = (x * lax.rsqrt(ms + EPS) * gmix_ref[...]).astype(BF16)

    nat = jnp.dot(h, wnat_ref[...], preferred_element_type=F32)
    k = nat[:, :D_SB]
    kk = k * k
    kk_hi = kk.astype(BF16)
    kk_lo = (kk - kk_hi.astype(F32)).astype(BF16)
    kss = (jnp.dot(kk_hi, bd_ref[...], preferred_element_type=F32)
           + jnp.dot(kk_lo, bd_ref[...], preferred_element_type=F32))
    k_ref[...] = (k * lax.rsqrt(kss * (1.0 / SB_HEAD_DIM) + EPS) * kg_ref[...]).astype(BF16)
    xrg_ref[...] = nat[:, D_SB:D_SB + D_RG]
    grg_ref[...] = nat[:, D_SB + D_RG:]

    t = x.shape[0]
    qt = _nt_dot(wtq_ref[...], h).reshape(SB_HEADS, SB_HEAD_DIM, t)
    qss = jnp.sum(qt * qt, axis=1, keepdims=True)
    qn = qt * lax.rsqrt(qss * (1.0 / SB_HEAD_DIM) + EPS)
    qn = qn.reshape(D_SB, t) * (qg_ref[...] * (SB_HEAD_DIM ** -0.5 * LOG2_E))
    qt_ref[...] = qn.astype(BF16)

    vt = _nt_dot(wtv_ref[...], h).astype(BF16)
    for c in range(t // ATT_K):
        vt_ref[c] = vt[:, c * ATT_K:(c + 1) * ATT_K]


def _inproj(x2d, gmix, wnat, wtq, wtv, qg, kg, bd):
    s = x2d.shape[0]
    t = INPROJ_TOKENS
    n_nat = wnat.shape[1]
    const = lambda shape: pl.BlockSpec(shape, lambda i: (0,) * len(shape))
    return pl.pallas_call(
        _inproj_kernel,
        grid=(s // t,),
        in_specs=[
            pl.BlockSpec((t, D_MODEL), lambda i: (i, 0)),
            const((1, D_MODEL)),
            const((D_MODEL, n_nat)),
            const((D_SB, D_MODEL)),
            const((D_SB, D_MODEL)),
            const((D_SB, 1)),
            const((1, D_SB)),
            const((D_SB, D_SB)),
        ],
        out_specs=[
            pl.BlockSpec((D_SB, t), lambda i: (0, i)),
            pl.BlockSpec((t, D_SB), lambda i: (i, 0)),
            pl.BlockSpec((t // ATT_K, D_SB, ATT_K), lambda i: (i, 0, 0)),
            pl.BlockSpec((t, D_RG), lambda i: (i, 0)),
            pl.BlockSpec((t, D_RG), lambda i: (i, 0)),
        ],
        out_shape=[
            jax.ShapeDtypeStruct((D_SB, s), BF16),
            jax.ShapeDtypeStruct((s, D_SB), BF16),
            jax.ShapeDtypeStruct((s // ATT_K, D_SB, ATT_K), BF16),
            jax.ShapeDtypeStruct((s, D_RG), F32),
            jax.ShapeDtypeStruct((s, D_RG), F32),
        ],
        compiler_params=pltpu.CompilerParams(
            dimension_semantics=("arbitrary",), vmem_limit_bytes=VMEM_LIMIT_BYTES),
        name="inproj",
    )(x2d, gmix, wnat, wtq, wtv, qg, kg, bd)


def _attn_kernel(qt_ref, k_ref, vt_ref, tri_ref, ot_ref, d_ref, s_ref, w_ref):
    hd = pl.program_id(0)
    j = pl.program_id(1)
    q = qt_ref[...]
    zero = jnp.zeros_like(q)
    qpad = jnp.where(hd % 2 == 0,
                     jnp.concatenate([q, zero], axis=0),
                     jnp.concatenate([zero, q], axis=0))
    tri = tri_ref[...]
    subs = ATT_Q // ATT_K
    q0 = j * ATT_Q

    def a_mm(sb):
        out = []
        for c in range(subs):
            ks = pl.multiple_of((sb * subs + c) * ATT_K, ATT_K)
            out.append(jnp.dot(k_ref[pl.ds(ks, ATT_K), :], qpad, preferred_element_type=F32))
        return out

    def a_ew(zs, sb, slot, masked):
        for c, z in enumerate(zs):
            pos = jnp.maximum(z, 0.0)
            neg = jnp.minimum(z, 0.0)
            soft = jnp.log2(1.0 + jnp.exp2(neg - pos))
            sp = pos + soft
            d = neg - soft
            if masked:
                kpos = (sb * subs + c) * ATT_K + lax.broadcasted_iota(jnp.int32, z.shape, 0)
                qpos = q0 + lax.broadcasted_iota(jnp.int32, z.shape, 1)
                valid = kpos < qpos
                sp = jnp.where(valid, sp, 0.0)
                d = jnp.where(valid, d, NEG_INF)
            d_ref[slot, c] = d
            s_ref[slot, c] = sp.astype(BF16)

    def b_mm(slot):
        return [jnp.dot(tri, s_ref[slot, c], preferred_element_type=F32) for c in range(subs)]

    def b_ew(sufs, slot, carry):
        for c in range(subs - 1, -1, -1):
            w_ref[slot, c] = jnp.exp2(d_ref[slot, c] - sufs[c] - carry).astype(BF16)
            carry = carry + (sufs[c][0:1, :] + s_ref[slot, c, 0:1, :].astype(F32))
        return carry

    def c_mm(sb, slot, acc):
        for c in range(subs):
            acc = acc + jnp.dot(vt_ref[sb * subs + c], w_ref[slot, c], preferred_element_type=F32)
        return acc

    def tick(n, slot, carry, acc, do_a=True, do_b=True, do_c=True):
        if do_a:
            zs = a_mm(j - n)
        if do_b:
            sufs = b_mm(1 - slot)
        if do_c:
            acc = c_mm(j - n + 2, slot, acc)
        if do_a:
            a_ew(zs, j - n, slot, False)
        if do_b:
            carry = b_ew(sufs, 1 - slot, carry)
        return carry, acc

    n_steps = j + 1
    carry = jnp.zeros((1, ATT_Q), F32)
    acc = jnp.zeros((SB_HEAD_DIM, ATT_Q), F32)
    a_ew(a_mm(j), j, 0, True)

    def single(ca):
        carry, acc = tick(1, 1, *ca, do_a=False, do_c=False)
        return tick(2, 0, carry, acc, do_a=False, do_b=False)

    def multi(ca):
        carry, acc = tick(1, 1, *ca, do_c=False)

        def body(p, ca):
            carry, acc = tick(2 * p + 2, 0, *ca)
            return tick(2 * p + 3, 1, carry, acc)

        carry, acc = lax.fori_loop(0, (n_steps - 2) // 2, body, (carry, acc))

        def tail_even(ca):
            carry, acc = tick(n_steps, 0, *ca, do_a=False)
            return tick(n_steps + 1, 1, carry, acc, do_a=False, do_b=False)

        def tail_odd(ca):
            carry, acc = tick(n_steps - 1, 0, *ca)
            carry, acc = tick(n_steps, 1, carry, acc, do_a=False)
            return tick(n_steps + 1, 0, carry, acc, do_a=False, do_b=False)

        return lax.cond(n_steps % 2 == 0, tail_even, tail_odd, (carry, acc))

    carry, acc = lax.cond(n_steps == 1, single, multi, (carry, acc))
    ot_ref[...] = acc


def _attention(qt, k, vt, tri):
    s = k.shape[0]
    return pl.pallas_call(
        _attn_kernel,
        grid=(SB_HEADS, s // ATT_Q),
        in_specs=[
            pl.BlockSpec((SB_HEAD_DIM, ATT_Q), lambda h, j: (h, j)),
            pl.BlockSpec((s, 2 * SB_HEAD_DIM), lambda h, j: (0, h // 2)),
            pl.BlockSpec((s // ATT_K, SB_HEAD_DIM, ATT_K), lambda h, j: (0, h, 0)),
            pl.BlockSpec((ATT_K, ATT_K), lambda h, j: (0, 0)),
        ],
        out_specs=pl.BlockSpec((SB_HEAD_DIM, ATT_Q), lambda h, j: (h, j)),
        out_shape=jax.ShapeDtypeStruct((D_SB, s), F32),
        scratch_shapes=[
            pltpu.VMEM((2, ATT_Q // ATT_K, ATT_K, ATT_Q), F32),
            pltpu.VMEM((2, ATT_Q // ATT_K, ATT_K, ATT_Q), BF16),
            pltpu.VMEM((2, ATT_Q // ATT_K, ATT_K, ATT_Q), BF16),
        ],
        compiler_params=pltpu.CompilerParams(
            dimension_semantics=("arbitrary", "arbitrary"), vmem_limit_bytes=VMEM_LIMIT_BYTES),
        name="sb_attention",
    )(qt, k, vt, tri)


def _mix_kernel(xrg_ref, grg_ref, ot_ref, x_ref, cw_ref, cb_ref, wa_ref, ba_ref, wx_ref, bx_ref,
                lam_ref, nsb_ref, nrg_ref, wout_ref, nffn_ref,
                x2_ref, h2_ref, xs_ref, hs_ref, hstate_ref):
    t = xrg_ref.shape[0]
    hist = SUBLANES

    @pl.when(pl.program_id(0) == 0)
    def _():
        xs_ref[0:hist, :] = jnp.zeros((hist, D_RG), F32)
        hstate_ref[...] = jnp.zeros_like(hstate_ref)

    xs_ref[hist:hist + t, :] = xrg_ref[...]
    y = cb_ref[...] + cw_ref[0:1, :] * xs_ref[hist - 3:hist - 3 + t, :]
    for jj in range(1, CONV_WIDTH):
        y = y + cw_ref[jj:jj + 1, :] * xs_ref[hist - 3 + jj:hist - 3 + jj + t, :]
    xs_ref[0:hist, :] = xs_ref[t:t + hist, :]

    yb = y.astype(BF16)
    r = jax.nn.sigmoid(jnp.dot(yb, wa_ref[...], preferred_element_type=F32) + ba_ref[...])
    ig = jax.nn.sigmoid(jnp.dot(yb, wx_ref[...], preferred_element_type=F32) + bx_ref[...])
    log_a = (-RG_C) * r * _softplus(-lam_ref[...])
    a = jnp.exp(log_a)
    b = jnp.sqrt(jnp.tanh(-log_a) * (1.0 + a * a)) * (ig * y)

    rowmod = lax.broadcasted_iota(jnp.int32, (t, D_RG), 0) % SUBLANES
    d = 1
    while d < SUBLANES:
        keep = rowmod >= d
        a_sh = pltpu.roll(a, d, axis=0)
        b_sh = pltpu.roll(b, d, axis=0)
        b = jnp.where(keep, b + a * b_sh, b)
        a = jnp.where(keep, a * a_sh, a)
        d *= 2
    hprev = hstate_ref[...]
    for g in range(t // SUBLANES):
        sl = slice(g * SUBLANES, (g + 1) * SUBLANES)
        hg = a[sl] * hprev + b[sl]
        hs_ref[sl, :] = hg
        hprev = jnp.broadcast_to(hg[SUBLANES - 1:SUBLANES, :], (SUBLANES, D_RG))
    hstate_ref[...] = hprev

    o_rg = hs_ref[...] * _gelu_tanh(grg_ref[...])
    o_sb = ot_ref[...].T

    def rms(v, g):
        return v * lax.rsqrt(jnp.mean(v * v, axis=-1, keepdims=True) + EPS) * g

    n_sb = rms(o_sb, nsb_ref[...]).astype(BF16)
    n_rg = rms(o_rg, nrg_ref[...]).astype(BF16)
    mix = (jnp.dot(n_sb, wout_ref[0:D_SB, :], preferred_element_type=F32)
           + jnp.dot(n_rg, wout_ref[D_SB:, :], preferred_element_type=F32))
    x2 = x_ref[...] + mix
    x2_ref[...] = x2
    h2_ref[...] = rms(x2, nffn_ref[...]).astype(BF16)


def _mix(xrg, grg, ot, x2d, cw, cb, wa, ba, wx, bx, lam, nsb, nrg, wout, nffn):
    s = x2d.shape[0]
    t = MIX_TOKENS
    const = lambda shape: pl.BlockSpec(shape, lambda i: (0,) * len(shape))
    return pl.pallas_call(
        _mix_kernel,
        grid=(s // t,),
        in_specs=[
            pl.BlockSpec((t, D_RG), lambda i: (i, 0)),
            pl.BlockSpec((t, D_RG), lambda i: (i, 0)),
            pl.BlockSpec((D_SB, t), lambda i: (0, i)),
            pl.BlockSpec((t, D_MODEL), lambda i: (i, 0)),
            const((CONV_WIDTH, D_RG)), const((1, D_RG)),
            const((D_RG, D_RG)), const((1, D_RG)),
            const((D_RG, D_RG)), const((1, D_RG)),
            const((1, D_RG)), const((1, D_SB)), const((1, D_RG)),
            const((D_MODEL, D_MODEL)), const((1, D_MODEL)),
        ],
        out_specs=[
            pl.BlockSpec((t, D_MODEL), lambda i: (i, 0)),
            pl.BlockSpec((t, D_MODEL), lambda i: (i, 0)),
        ],
        out_shape=[
            jax.ShapeDtypeStruct((s, D_MODEL), F32),
            jax.ShapeDtypeStruct((s, D_MODEL), BF16),
        ],
        scratch_shapes=[
            pltpu.VMEM((t + SUBLANES, D_RG), F32),
            pltpu.VMEM((t, D_RG), F32),
            pltpu.VMEM((SUBLANES, D_RG), F32),
        ],
        compiler_params=pltpu.CompilerParams(
            dimension_semantics=("arbitrary",), vmem_limit_bytes=VMEM_LIMIT_BYTES),
        name="rglru_outproj",
    )(xrg, grg, ot, x2d, cw, cb, wa, ba, wx, bx, lam, nsb, nrg, wout, nffn)


_N_RANK = PEER_TOPK + 1
_CAND_PAIRS = [(a, b) for a in range(_N_RANK) for b in range(_N_RANK)
               if (a + 1) * (b + 1) <= _N_RANK]


def _top_desc(s, n, with_rank=False):
    out = []
    rank = jnp.full(s.shape, float(s.shape[0]), F32) if with_rank else None
    for r in range(n):
        m = jnp.max(s, axis=0, keepdims=True)
        out.append(m)
        hit = s == m
        if with_rank and r < n - 1:
            rank = jnp.where(hit, float(r), rank)
        s = jnp.where(hit, NEG_INF, s)
    return (out, rank) if with_rank else out


def _peer_route(h, wq_ref, keys_ref, e1_ref, n1_ref, e2_ref, r2_ref):
    for hd in range(PEER_HEADS):
        qt = _nt_dot(wq_ref[hd * PEER_D_KEY:(hd + 1) * PEER_D_KEY, :], h).astype(BF16)
        s1 = jnp.dot(keys_ref[hd, 0], qt[:PEER_HALF], preferred_element_type=F32)
        s2 = jnp.dot(keys_ref[hd, 1], qt[PEER_HALF:], preferred_element_type=F32)
        top1 = _top_desc(s1, _N_RANK)
        top2, rank2 = _top_desc(s2, _N_RANK, with_rank=True)
        cand = jnp.concatenate([top1[a] + top2[b] for a, b in _CAND_PAIRS], axis=0)
        best = _top_desc(cand, _N_RANK)
        zsum = jnp.ones_like(best[0])
        for kk in range(1, PEER_TOPK):
            zsum = zsum + jnp.exp(best[kk] - best[0])
        tau = 0.5 * (best[PEER_TOPK - 1] + best[PEER_TOPK])
        theta = tau - s1
        count1 = jnp.zeros_like(s1)
        for r in range(PEER_TOPK):
            count1 = count1 + jnp.where(top2[r] > theta, 1.0, 0.0)
        e1_ref[hd] = jnp.exp(s1 - top1[0]) / zsum
        n1_ref[hd] = count1
        e2_ref[hd] = jnp.exp(s2 - top2[0]).astype(BF16)
        r2_ref[hd] = rank2.astype(BF16)


def _peer_kernel(h_ref, x2_ref, wq_ref, keys_ref, u_ref, vt_ref, out_ref,
                 e1_ref, n1_ref, e2_ref, r2_ref, p_ref, acc_ref):
    e = pl.program_id(1)
    t = h_ref.shape[0]
    n1 = PEER_EXPERTS // PEER_N_KEYS
    assert n1 == SUBLANES

    @pl.when(e == 0)
    def _():
        _peer_route(h_ref[...], wq_ref, keys_ref, e1_ref, n1_ref, e2_ref, r2_ref)
        acc_ref[...] = jnp.zeros_like(acc_ref)

    i1_base = pl.multiple_of(e * n1, SUBLANES)
    h = h_ref[...]
    il_per_chunk = PEER_CHUNK // PEER_N_KEYS
    rows = 2 * SUBLANES
    n_row_tiles = PEER_N_KEYS // rows
    n_chunks = PEER_EXPERTS // PEER_CHUNK
    chunk = lambda ch: slice(ch * PEER_CHUNK, (ch + 1) * PEER_CHUNK)
    pre = [_nt_dot(u_ref[chunk(ch), :], h) for ch in range(n_chunks)]
    upd = None
    for ch in range(n_chunks + 1):
        if ch > 0:
            pc = chunk(ch - 1)
            d = jnp.dot(vt_ref[:, pc], p_ref[pc, :], preferred_element_type=F32)
            upd = d if upd is None else upd + d
        if ch == n_chunks:
            break
        for tc in range(t // LANES):
            ls = slice(tc * LANES, (tc + 1) * LANES)
            w = [[jnp.zeros((rows, LANES), BF16) for _ in range(n_row_tiles)]
                 for _ in range(il_per_chunk)]
            for hd in range(PEER_HEADS):
                cnt8 = n1_ref[hd, pl.ds(i1_base, n1), ls]
                e18 = e1_ref[hd, pl.ds(i1_base, n1), ls]
                cnt, e1 = [], []
                for ii in range(il_per_chunk):
                    il = ch * il_per_chunk + ii
                    cnt.append(jnp.broadcast_to(cnt8[il:il + 1], (rows, LANES)).astype(BF16))
                    e1.append(jnp.broadcast_to(e18[il:il + 1], (rows, LANES)).astype(BF16))
                for rt in range(n_row_tiles):
                    rs = slice(rt * rows, (rt + 1) * rows)
                    r2 = r2_ref[hd, rs, ls]
                    e2 = e2_ref[hd, rs, ls]
                    for ii in range(il_per_chunk):
                        w[ii][rt] = w[ii][rt] + jnp.where(r2 < cnt[ii], e1[ii] * e2, 0.0)
            act = _gelu_tanh(pre[ch][:, ls]).astype(BF16)
            for ii in range(il_per_chunk):
                for rt in range(n_row_tiles):
                    r0 = ii * PEER_N_KEYS + rt * rows
                    p_ref[ch * PEER_CHUNK + r0:ch * PEER_CHUNK + r0 + rows, ls] = (
                        w[ii][rt] * act[r0:r0 + rows])
    acc_ref[...] += upd

    @pl.when(e == pl.num_programs(1) - 1)
    def _():
        out_ref[...] = x2_ref[...] + acc_ref[...].T


def _peer(h2, x2, wq_t, keys, u_bf, vt_bf):
    s = h2.shape[0]
    t = PEER_TOKENS
    n_exp = u_bf.shape[0]
    et = PEER_EXPERTS
    rt = lambda dtype: pltpu.VMEM((PEER_HEADS, PEER_N_KEYS, t), dtype)
    return pl.pallas_call(
        _peer_kernel,
        grid=(s // t, n_exp // et),
        in_specs=[
            pl.BlockSpec((t, D_MODEL), lambda i, e: (i, 0)),
            pl.BlockSpec((t, D_MODEL), lambda i, e: (i, 0)),
            pl.BlockSpec((PEER_HEADS * PEER_D_KEY, D_MODEL), lambda i, e: (0, 0)),
            pl.BlockSpec((PEER_HEADS, 2, PEER_N_KEYS, PEER_HALF), lambda i, e: (0, 0, 0, 0)),
            pl.BlockSpec((et, D_MODEL), lambda i, e: (e, 0)),
            pl.BlockSpec((D_MODEL, et), lambda i, e: (0, e)),
        ],
        out_specs=pl.BlockSpec((t, D_MODEL), lambda i, e: (i, 0)),
        out_shape=jax.ShapeDtypeStruct((s, D_MODEL), F32),
        scratch_shapes=[rt(F32), rt(F32), rt(BF16), rt(BF16),
                        pltpu.VMEM((et, t), BF16),
                        pltpu.VMEM((D_MODEL, t), F32)],
        compiler_params=pltpu.CompilerParams(
            dimension_semantics=("arbitrary", "arbitrary"), vmem_limit_bytes=VMEM_LIMIT_BYTES),
        name="peer",
    )(h2, x2, wq_t, keys, u_bf, vt_bf)


def _block_diag(w):
    n, bi, bj = w.shape
    eye = jnp.eye(n, dtype=w.dtype)
    return (eye[:, None, :, None] * w[:, :, None, :]).reshape(n * bi, n * bj)


def kernel(x, norm_mix, w_in, q_norm, k_norm, conv_w, conv_b, rg_w_a, rg_b_a, rg_w_x, rg_b_x,
           rg_lambda, out_norm_sb, out_norm_rg, w_out, norm_ffn, peer_w_query, peer_sub_keys,
           peer_u, peer_v):
    bsz, s, d = x.shape
    assert bsz == 1 and d == D_MODEL
    assert s % max(INPROJ_TOKENS, ATT_Q, MIX_TOKENS, PEER_TOKENS) == 0
    depth = w_in.shape[0]
    x2d = x.reshape(s, d)

    head_id = jnp.arange(D_SB) // SB_HEAD_DIM
    bd = (head_id[:, None] == head_id[None, :]).astype(BF16)
    kidx = jnp.arange(ATT_K)
    tri = (kidx[None, :] > kidx[:, None]).astype(BF16)
    row = lambda v: v.reshape(1, -1)

    for l in range(depth):
        w = w_in[l]
        wtq = w[:, 0:D_SB].T.astype(BF16)
        wtv = w[:, 2 * D_SB:3 * D_SB].T.astype(BF16)
        wnat = jnp.concatenate([w[:, D_SB:2 * D_SB], w[:, 3 * D_SB:]], axis=1).astype(BF16)
        qg = jnp.tile(q_norm[l], SB_HEADS).reshape(D_SB, 1)
        kg = jnp.tile(k_norm[l], SB_HEADS).reshape(1, D_SB)
        qt, k, vt, xrg, grg = _inproj(x2d, row(norm_mix[l]), wnat, wtq, wtv, qg, kg, bd)

        ot = _attention(qt, k, vt, tri)

        x2, h2 = _mix(xrg, grg, ot, x2d, conv_w[l], row(conv_b[l]),
                      _block_diag(rg_w_a[l]).astype(BF16), row(rg_b_a[l]),
                      _block_diag(rg_w_x[l]).astype(BF16), row(rg_b_x[l]),
                      row(rg_lambda[l]), row(out_norm_sb[l]), row(out_norm_rg[l]),
                      w_out[l].astype(BF16), row(norm_ffn[l]))

        wq_t = peer_w_query[l].reshape(d, PEER_HEADS * PEER_D_KEY).T.astype(BF16)
        x2d = _peer(h2, x2, wq_t, peer_sub_keys[l].astype(BF16),
                    peer_u[l].astype(BF16), peer_v[l].T.astype(BF16))
    return x2d.reshape(bsz, s, d)
```

```python
import functools

import jax
import jax.numpy as jnp
from jax import lax
from jax.experimental import pallas as pl
from jax.experimental.pallas import tpu as pltpu

F32 = jnp.float32
BF16 = jnp.bfloat16

D_MODEL = 1024
SB_HEADS = 8
SB_HEAD_DIM = 64
D_SB = SB_HEADS * SB_HEAD_DIM
D_RG = D_MODEL - D_SB
RG_BLOCKS = 8
RG_BLOCK_DIM = D_RG // RG_BLOCKS
CONV_WIDTH = 4
RG_C = 8.0
PEER_HEADS = 8
PEER_N_KEYS = 128
PEER_D_KEY = 256
PEER_HALF = PEER_D_KEY // 2
PEER_TOPK = 16
EPS = 1e-6

VMEM_LIMIT_BYTES = 56 * 1024 * 1024
SUBLANES = 8
LANES = 128

INPROJ_TOKENS = 512
ATT_Q = 512
ATT_K = 256
MIX_TOKENS = 256
PEER_TOKENS = 512
PEER_EXPERTS = 1024
PEER_CHUNK = 256

NEG_INF = float("-inf")
LOG2_E = 1.4426950408889634


def _nt_dot(a, b):
    return lax.dot_general(a, b, (((1,), (1,)), ((), ())), preferred_element_type=F32)


def _softplus(z):
    return jnp.maximum(z, 0.0) + jnp.log(1.0 + jnp.exp(-jnp.abs(z)))


def _gelu_tanh(x):
    c0 = 0.7978845608028654
    c1 = 0.044715 * c0
    half = 0.5 * x
    return half + half * jnp.tanh(x * (c1 * (x * x) + c0))


def _inproj_kernel(x_ref, gmix_ref, wnat_ref, wtq_ref, wtv_ref, qg_ref, kg_ref, bd_ref,
                   qt_ref, k_ref, vt_ref, xrg_ref, grg_ref):
    x = x_ref[...]
    ms = jnp.mean(x * x, axis=-1, keepdims=True)
    h = (x * lax.rsqrt(ms + EPS) * gmix_ref[...]).astype(BF16)

    nat = jnp.dot(h, wnat_ref[...], preferred_element_type=F32)
    k = nat[:, :D_SB]
    kk = k * k
    kk_hi = kk.astype(BF16)
    kk_lo = (kk - kk_hi.astype(F32)).astype(BF16)
    kss = (jnp.dot(kk_hi, bd_ref[...], preferred_element_type=F32)
           + jnp.dot(kk_lo, bd_ref[...], preferred_element_type=F32))
    k_ref[...] = (k * lax.rsqrt(kss * (1.0 / SB_HEAD_DIM) + EPS) * kg_ref[...]).astype(BF16)
    xrg_ref[...] = nat[:, D_SB:D_SB + D_RG]
    grg_ref[...] = nat[:, D_SB + D_RG:]

    t = x.shape[0]
    qt = _nt_dot(wtq_ref[...], h).reshape(SB_HEADS, SB_HEAD_DIM, t)
    qss = jnp.sum(qt * qt, axis=1, keepdims=True)
    qn = qt * lax.rsqrt(qss * (1.0 / SB_HEAD_DIM) + EPS)
    qn = qn.reshape(D_SB, t) * (qg_ref[...] * (SB_HEAD_DIM ** -0.5 * LOG2_E))
    qt_ref[...] = qn.astype(BF16)

    vt = _nt_dot(wtv_ref[...], h).astype(BF16)
    for c in range(t // ATT_K):
        vt_ref[c] = vt[:, c * ATT_K:(c + 1) * ATT_K]


def _inproj(x2d, gmix, wnat, wtq, wtv, qg, kg, bd):
    s = x2d.shape[0]
    t = INPROJ_TOKENS
    n_nat = wnat.shape[1]
    const = lambda shape: pl.BlockSpec(shape, lambda i: (0,) * len(shape))
    return pl.pallas_call(
        _inproj_kernel,
        grid=(s // t,),
        in_specs=[
            pl.BlockSpec((t, D_MODEL), lambda i: (i, 0)),
            const((1, D_MODEL)),
            const((D_MODEL, n_nat)),
            const((D_SB, D_MODEL)),
            const((D_SB, D_MODEL)),
            const((D_SB, 1)),
            const((1, D_SB)),
            const((D_SB, D_SB)),
        ],
        out_specs=[
            pl.BlockSpec((D_SB, t), lambda i: (0, i)),
            pl.BlockSpec((t, D_SB), lambda i: (i, 0)),
            pl.BlockSpec((t // ATT_K, D_SB, ATT_K), lambda i: (i, 0, 0)),
            pl.BlockSpec((t, D_RG), lambda i: (i, 0)),
            pl.BlockSpec((t, D_RG), lambda i: (i, 0)),
        ],
        out_shape=[
            jax.ShapeDtypeStruct((D_SB, s), BF16),
            jax.ShapeDtypeStruct((s, D_SB), BF16),
            jax.ShapeDtypeStruct((s // ATT_K, D_SB, ATT_K), BF16),
            jax.ShapeDtypeStruct((s, D_RG), F32),
            jax.ShapeDtypeStruct((s, D_RG), F32),
        ],
        compiler_params=pltpu.CompilerParams(
            dimension_semantics=("arbitrary",), vmem_limit_bytes=VMEM_LIMIT_BYTES),
        name="inproj",
    )(x2d, gmix, wnat, wtq, wtv, qg, kg, bd)


def _attn_kernel(qt_ref, k_ref, vt_ref, tri_ref, ot_ref, d_ref, s_ref, w_ref):
    hd = pl.program_id(0)
    j = pl.program_id(1)
    q = qt_ref[...]
    zero = jnp.zeros_like(q)
    qpad = jnp.where(hd % 2 == 0,
                     jnp.concatenate([q, zero], axis=0),
                     jnp.concatenate([zero, q], axis=0))
    tri = tri_ref[...]
    subs = ATT_Q // ATT_K
    q0 = j * ATT_Q

    def a_mm(sb):
        out = []
        for c in range(subs):
            ks = pl.multiple_of((sb * subs + c) * ATT_K, ATT_K)
            out.append(jnp.dot(k_ref[pl.ds(ks, ATT_K), :], qpad, preferred_element_type=F32))
        return out

    def a_ew(zs, sb, slot, masked):
        for c, z in enumerate(zs):
            pos = jnp.maximum(z, 0.0)
            neg = jnp.minimum(z, 0.0)
            soft = jnp.log2(1.0 + jnp.exp2(neg - pos))
            sp = pos + soft
            d = neg - soft
            if masked:
                kpos = (sb * subs + c) * ATT_K + lax.broadcasted_iota(jnp.int32, z.shape, 0)
                qpos = q0 + lax.broadcasted_iota(jnp.int32, z.shape, 1)
                valid = kpos < qpos
                sp = jnp.where(valid, sp, 0.0)
                d = jnp.where(valid, d, NEG_INF)
            d_ref[slot, c] = d
            s_ref[slot, c] = sp.astype(BF16)

    def b_mm(slot):
        return [jnp.dot(tri, s_ref[slot, c], preferred_element_type=F32) for c in range(subs)]

    def b_ew(sufs, slot, carry):
        for c in range(subs - 1, -1, -1):
            w_ref[slot, c] = jnp.exp2(d_ref[slot, c] - sufs[c] - carry).astype(BF16)
            carry = carry + (sufs[c][0:1, :] + s_ref[slot, c, 0:1, :].astype(F32))
        return carry

    def c_mm(sb, slot, acc):
        for c in range(subs):
            acc = acc + jnp.dot(vt_ref[sb * subs + c], w_ref[slot, c], preferred_element_type=F32)
        return acc

    def tick(n, slot, carry, acc, do_a=True, do_b=True, do_c=True):
        if do_a:
            zs = a_mm(j - n)
        if do_b:
            sufs = b_mm(1 - slot)
        if do_c:
            acc = c_mm(j - n + 2, slot, acc)
        if do_a:
            a_ew(zs, j - n, slot, False)
        if do_b:
            carry = b_ew(sufs, 1 - slot, carry)
        return carry, acc

    n_steps = j + 1
    carry = jnp.zeros((1, ATT_Q), F32)
    acc = jnp.zeros((SB_HEAD_DIM, ATT_Q), F32)
    a_ew(a_mm(j), j, 0, True)

    def single(ca):
        carry, acc = tick(1, 1, *ca, do_a=False, do_c=False)
        return tick(2, 0, carry, acc, do_a=False, do_b=False)

    def multi(ca):
        carry, acc = tick(1, 1, *ca, do_c=False)

        def body(p, ca):
            carry, acc = tick(2 * p + 2, 0, *ca)
            return tick(2 * p + 3, 1, carry, acc)

        carry, acc = lax.fori_loop(0, (n_steps - 2) // 2, body, (carry, acc))

        def tail_even(ca):
            carry, acc = tick(n_steps, 0, *ca, do_a=False)
            return tick(n_steps + 1, 1, carry, acc, do_a=False, do_b=False)

        def tail_odd(ca):
            carry, acc = tick(n_steps - 1, 0, *ca)
            carry, acc = tick(n_steps, 1, carry, acc, do_a=False)
            return tick(n_steps + 1, 0, carry, acc, do_a=False, do_b=False)

        return lax.cond(n_steps % 2 == 0, tail_even, tail_odd, (carry, acc))

    carry, acc = lax.cond(n_steps == 1, single, multi, (carry, acc))
    ot_ref[...] = acc


def _attention(qt, k, vt, tri):
    s = k.shape[0]
    return pl.pallas_call(
        _attn_kernel,
        grid=(SB_HEADS, s // ATT_Q),
        in_specs=[
            pl.BlockSpec((SB_HEAD_DIM, ATT_Q), lambda h, j: (h, j)),
            pl.BlockSpec((s, 2 * SB_HEAD_DIM), lambda h, j: (0, h // 2)),
            pl.BlockSpec((s // ATT_K, SB_HEAD_DIM, ATT_K), lambda h, j: (0, h, 0)),
            pl.BlockSpec((ATT_K, ATT_K), lambda h, j: (0, 0)),
        ],
        out_specs=pl.BlockSpec((SB_HEAD_DIM, ATT_Q), lambda h, j: (h, j)),
        out_shape=jax.ShapeDtypeStruct((D_SB, s), F32),
        scratch_shapes=[
            pltpu.VMEM((2, ATT_Q // ATT_K, ATT_K, ATT_Q), F32),
            pltpu.VMEM((2, ATT_Q // ATT_K, ATT_K, ATT_Q), BF16),
            pltpu.VMEM((2, ATT_Q // ATT_K, ATT_K, ATT_Q), BF16),
        ],
        compiler_params=pltpu.CompilerParams(
            dimension_semantics=("arbitrary", "arbitrary"), vmem_limit_bytes=VMEM_LIMIT_BYTES),
        name="sb_attention",
    )(qt, k, vt, tri)


def _mix_kernel(xrg_ref, grg_ref, ot_ref, x_ref, cw_ref, cb_ref, wa_ref, ba_ref, wx_ref, bx_ref,
                lam_ref, nsb_ref, nrg_ref, wout_ref, nffn_ref,
                x2_ref, h2t_ref, xs_ref, hs_ref, hstate_ref):
    t = xrg_ref.shape[0]
    hist = SUBLANES

    @pl.when(pl.program_id(0) == 0)
    def _():
        xs_ref[0:hist, :] = jnp.zeros((hist, D_RG), F32)
        hstate_ref[...] = jnp.zeros_like(hstate_ref)

    xs_ref[hist:hist + t, :] = xrg_ref[...]
    y = cb_ref[...] + cw_ref[0:1, :] * xs_ref[hist - 3:hist - 3 + t, :]
    for jj in range(1, CONV_WIDTH):
        y = y + cw_ref[jj:jj + 1, :] * xs_ref[hist - 3 + jj:hist - 3 + jj + t, :]
    xs_ref[0:hist, :] = xs_ref[t:t + hist, :]

    yb = y.astype(BF16)
    r = jax.nn.sigmoid(jnp.dot(yb, wa_ref[...], preferred_element_type=F32) + ba_ref[...])
    ig = jax.nn.sigmoid(jnp.dot(yb, wx_ref[...], preferred_element_type=F32) + bx_ref[...])
    log_a = (-RG_C) * r * _softplus(-lam_ref[...])
    a = jnp.exp(log_a)
    b = jnp.sqrt(jnp.tanh(-log_a) * (1.0 + a * a)) * (ig * y)

    rowmod = lax.broadcasted_iota(jnp.int32, (t, D_RG), 0) % SUBLANES
    d = 1
    while d < SUBLANES:
        keep = rowmod >= d
        a_sh = pltpu.roll(a, d, axis=0)
        b_sh = pltpu.roll(b, d, axis=0)
        b = jnp.where(keep, b + a * b_sh, b)
        a = jnp.where(keep, a * a_sh, a)
        d *= 2
    hprev = hstate_ref[...]
    for g in range(t // SUBLANES):
        sl = slice(g * SUBLANES, (g + 1) * SUBLANES)
        hg = a[sl] * hprev + b[sl]
        hs_ref[sl, :] = hg
        hprev = jnp.broadcast_to(hg[SUBLANES - 1:SUBLANES, :], (SUBLANES, D_RG))
    hstate_ref[...] = hprev

    o_rg = hs_ref[...] * _gelu_tanh(grg_ref[...])
    o_sb = ot_ref[...].T

    def rms(v, g):
        return v * lax.rsqrt(jnp.mean(v * v, axis=-1, keepdims=True) + EPS) * g

    n_sb = rms(o_sb, nsb_ref[...]).astype(BF16)
    n_rg = rms(o_rg, nrg_ref[...]).astype(BF16)
    mix = (jnp.dot(n_sb, wout_ref[0:D_SB, :], preferred_element_type=F32)
           + jnp.dot(n_rg, wout_ref[D_SB:, :], preferred_element_type=F32))
    x2 = x_ref[...] + mix
    x2_ref[...] = x2
    h2t_ref[...] = rms(x2, nffn_ref[...]).T.astype(BF16)


def _mix(xrg, grg, ot, x2d, cw, cb, wa, ba, wx, bx, lam, nsb, nrg, wout, nffn):
    s = x2d.shape[0]
    t = MIX_TOKENS
    const = lambda shape: pl.BlockSpec(shape, lambda i: (0,) * len(shape))
    return pl.pallas_call(
        _mix_kernel,
        grid=(s // t,),
        in_specs=[
            pl.BlockSpec((t, D_RG), lambda i: (i, 0)),
            pl.BlockSpec((t, D_RG), lambda i: (i, 0)),
            pl.BlockSpec((D_SB, t), lambda i: (0, i)),
            pl.BlockSpec((t, D_MODEL), lambda i: (i, 0)),
            const((CONV_WIDTH, D_RG)), const((1, D_RG)),
            const((D_RG, D_RG)), const((1, D_RG)),
            const((D_RG, D_RG)), const((1, D_RG)),
            const((1, D_RG)), const((1, D_SB)), const((1, D_RG)),
            const((D_MODEL, D_MODEL)), const((1, D_MODEL)),
        ],
        out_specs=[
            pl.BlockSpec((t, D_MODEL), lambda i: (i, 0)),
            pl.BlockSpec((D_MODEL, t), lambda i: (0, i)),
        ],
        out_shape=[
            jax.ShapeDtypeStruct((s, D_MODEL), F32),
            jax.ShapeDtypeStruct((D_MODEL, s), BF16),
        ],
        scratch_shapes=[
            pltpu.VMEM((t + SUBLANES, D_RG), F32),
            pltpu.VMEM((t, D_RG), F32),
            pltpu.VMEM((SUBLANES, D_RG), F32),
        ],
        compiler_params=pltpu.CompilerParams(
            dimension_semantics=("arbitrary",), vmem_limit_bytes=VMEM_LIMIT_BYTES),
        name="rglru_outproj",
    )(xrg, grg, ot, x2d, cw, cb, wa, ba, wx, bx, lam, nsb, nrg, wout, nffn)


_N_RANK = PEER_TOPK + 1
_CAND_PAIRS = [(a, b) for a in range(_N_RANK) for b in range(_N_RANK)
               if (a + 1) * (b + 1) <= _N_RANK]


def _top_desc(s, n, with_rank=False):
    out = []
    rank = jnp.full(s.shape, float(s.shape[0]), F32) if with_rank else None
    for r in range(n):
        m = jnp.max(s, axis=0, keepdims=True)
        out.append(m)
        hit = s == m
        if with_rank and r < n - 1:
            rank = jnp.where(hit, float(r), rank)
        s = jnp.where(hit, NEG_INF, s)
    return (out, rank) if with_rank else out


def _peer_route(ht, wq_ref, keys_ref, e1_ref, n1_ref, e2_ref, r2_ref):
    for hd in range(PEER_HEADS):
        qt = jnp.dot(wq_ref[hd * PEER_D_KEY:(hd + 1) * PEER_D_KEY, :], ht,
                     preferred_element_type=F32).astype(BF16)
        s1 = jnp.dot(keys_ref[hd, 0], qt[:PEER_HALF], preferred_element_type=F32)
        s2 = jnp.dot(keys_ref[hd, 1], qt[PEER_HALF:], preferred_element_type=F32)
        top1 = _top_desc(s1, _N_RANK)
        top2, rank2 = _top_desc(s2, _N_RANK, with_rank=True)
        cand = jnp.concatenate([top1[a] + top2[b] for a, b in _CAND_PAIRS], axis=0)
        best = _top_desc(cand, _N_RANK)
        zsum = jnp.ones_like(best[0])
        for kk in range(1, PEER_TOPK):
            zsum = zsum + jnp.exp(best[kk] - best[0])
        tau = 0.5 * (best[PEER_TOPK - 1] + best[PEER_TOPK])
        theta = tau - s1
        count1 = jnp.zeros_like(s1)
        for r in range(PEER_TOPK):
            count1 = count1 + jnp.where(top2[r] > theta, 1.0, 0.0)
        e1_ref[hd] = jnp.exp(s1 - top1[0]) / zsum
        n1_ref[hd] = count1
        e2_ref[hd] = jnp.exp(s2 - top2[0]).astype(BF16)
        r2_ref[hd] = rank2.astype(BF16)


def _peer_kernel(ht_ref, x2_ref, wq_ref, keys_ref, u_ref, vt_ref, out_ref,
                 e1_ref, n1_ref, e2_ref, r2_ref, p_ref, acc_ref):
    e = pl.program_id(1)
    t = ht_ref.shape[1]
    n1 = PEER_EXPERTS // PEER_N_KEYS
    assert n1 == SUBLANES

    @pl.when(e == 0)
    def _():
        _peer_route(ht_ref[...], wq_ref, keys_ref, e1_ref, n1_ref, e2_ref, r2_ref)
        acc_ref[...] = jnp.zeros_like(acc_ref)

    i1_base = pl.multiple_of(e * n1, SUBLANES)
    ht = ht_ref[...]
    il_per_chunk = PEER_CHUNK // PEER_N_KEYS
    rows = 2 * SUBLANES
    n_row_tiles = PEER_N_KEYS // rows
    n_chunks = PEER_EXPERTS // PEER_CHUNK
    chunk = lambda ch: slice(ch * PEER_CHUNK, (ch + 1) * PEER_CHUNK)
    pre = [jnp.dot(u_ref[chunk(ch), :], ht, preferred_element_type=F32)
           for ch in range(n_chunks)]
    upd = None
    for ch in range(n_chunks + 1):
        if ch > 0:
            pc = chunk(ch - 1)
            d = jnp.dot(vt_ref[:, pc], p_ref[pc, :], preferred_element_type=F32)
            upd = d if upd is None else upd + d
        if ch == n_chunks:
            break
        for tc in range(t // LANES):
            ls = slice(tc * LANES, (tc + 1) * LANES)
            w = [[jnp.zeros((rows, LANES), BF16) for _ in range(n_row_tiles)]
                 for _ in range(il_per_chunk)]
            for hd in range(PEER_HEADS):
                cnt8 = n1_ref[hd, pl.ds(i1_base, n1), ls]
                e18 = e1_ref[hd, pl.ds(i1_base, n1), ls]
                cnt, e1 = [], []
                for ii in range(il_per_chunk):
                    il = ch * il_per_chunk + ii
                    cnt.append(jnp.broadcast_to(cnt8[il:il + 1], (rows, LANES)).astype(BF16))
                    e1.append(jnp.broadcast_to(e18[il:il + 1], (rows, LANES)).astype(BF16))
                for rt in range(n_row_tiles):
                    rs = slice(rt * rows, (rt + 1) * rows)
                    r2 = r2_ref[hd, rs, ls]
                    e2 = e2_ref[hd, rs, ls]
                    for ii in range(il_per_chunk):
                        w[ii][rt] = w[ii][rt] + jnp.where(r2 < cnt[ii], e1[ii] * e2, 0.0)
            act = _gelu_tanh(pre[ch][:, ls]).astype(BF16)
            for ii in range(il_per_chunk):
                for rt in range(n_row_tiles):
                    r0 = ii * PEER_N_KEYS + rt * rows
                    p_ref[ch * PEER_CHUNK + r0:ch * PEER_CHUNK + r0 + rows, ls] = (
                        w[ii][rt] * act[r0:r0 + rows])
    acc_ref[...] += upd

    @pl.when(e == pl.num_programs(1) - 1)
    def _():
        out_ref[...] = x2_ref[...] + acc_ref[...].T


def _peer(h2t, x2, wq_t, keys, u_bf, vt_bf):
    s = h2t.shape[1]
    t = PEER_TOKENS
    n_exp = u_bf.shape[0]
    et = PEER_EXPERTS
    rt = lambda dtype: pltpu.VMEM((PEER_HEADS, PEER_N_KEYS, t), dtype)
    return pl.pallas_call(
        _peer_kernel,
        grid=(s // t, n_exp // et),
        in_specs=[
            pl.BlockSpec((D_MODEL, t), lambda i, e: (0, i)),
            pl.BlockSpec((t, D_MODEL), lambda i, e: (i, 0)),
            pl.BlockSpec((PEER_HEADS * PEER_D_KEY, D_MODEL), lambda i, e: (0, 0)),
            pl.BlockSpec((PEER_HEADS, 2, PEER_N_KEYS, PEER_HALF), lambda i, e: (0, 0, 0, 0)),
            pl.BlockSpec((et, D_MODEL), lambda i, e: (e, 0)),
            pl.BlockSpec((D_MODEL, et), lambda i, e: (0, e)),
        ],
        out_specs=pl.BlockSpec((t, D_MODEL), lambda i, e: (i, 0)),
        out_shape=jax.ShapeDtypeStruct((s, D_MODEL), F32),
        scratch_shapes=[rt(F32), rt(F32), rt(BF16), rt(BF16),
                        pltpu.VMEM((et, t), BF16),
                        pltpu.VMEM((D_MODEL, t), F32)],
        compiler_params=pltpu.CompilerParams(
            dimension_semantics=("arbitrary", "arbitrary"), vmem_limit_bytes=VMEM_LIMIT_BYTES),
        name="peer",
    )(h2t, x2, wq_t, keys, u_bf, vt_bf)


def _block_diag(w):
    n, bi, bj = w.shape
    eye = jnp.eye(n, dtype=w.dtype)
    return (eye[:, None, :, None] * w[:, :, None, :]).reshape(n * bi, n * bj)


def kernel(x, norm_mix, w_in, q_norm, k_norm, conv_w, conv_b, rg_w_a, rg_b_a, rg_w_x, rg_b_x,
           rg_lambda, out_norm_sb, out_norm_rg, w_out, norm_ffn, peer_w_query, peer_sub_keys,
           peer_u, peer_v):
    bsz, s, d = x.shape
    assert bsz == 1 and d == D_MODEL
    assert s % max(INPROJ_TOKENS, ATT_Q, MIX_TOKENS, PEER_TOKENS) == 0
    depth = w_in.shape[0]
    x2d = x.reshape(s, d)

    head_id = jnp.arange(D_SB) // SB_HEAD_DIM
    bd = (head_id[:, None] == head_id[None, :]).astype(BF16)
    kidx = jnp.arange(ATT_K)
    tri = (kidx[None, :] > kidx[:, None]).astype(BF16)
    row = lambda v: v.reshape(1, -1)

    for l in range(depth):
        w = w_in[l]
        wtq = w[:, 0:D_SB].T.astype(BF16)
        wtv = w[:, 2 * D_SB:3 * D_SB].T.astype(BF16)
        wnat = jnp.concatenate([w[:, D_SB:2 * D_SB], w[:, 3 * D_SB:]], axis=1).astype(BF16)
        qg = jnp.tile(q_norm[l], SB_HEADS).reshape(D_SB, 1)
        kg = jnp.tile(k_norm[l], SB_HEADS).reshape(1, D_SB)
        qt, k, vt, xrg, grg = _inproj(x2d, row(norm_mix[l]), wnat, wtq, wtv, qg, kg, bd)

        ot = _attention(qt, k, vt, tri)

        x2, h2t = _mix(xrg, grg, ot, x2d, conv_w[l], row(conv_b[l]),
                      _block_diag(rg_w_a[l]).astype(BF16), row(rg_b_a[l]),
                      _block_diag(rg_w_x[l]).astype(BF16), row(rg_b_x[l]),
                      row(rg_lambda[l]), row(out_norm_sb[l]), row(out_norm_rg[l]),
                      w_out[l].astype(BF16), row(norm_ffn[l]))

        wq_t = peer_w_query[l].reshape(d, PEER_HEADS * PEER_D_KEY).T.astype(BF16)
        x2d = _peer(h2t, x2, wq_t, peer_sub_keys[l].astype(BF16),
                    peer_u[l].astype(BF16), peer_v[l].T.astype(BF16))
    return x2d.reshape(bsz, s, d)
```

```python
import functools

import jax
import jax.numpy as jnp
from jax import lax
from jax.experimental import pallas as pl
from jax.experimental.pallas import tpu as pltpu

F32 = jnp.float32
BF16 = jnp.bfloat16

D_MODEL = 1024
SB_HEADS = 8
SB_HEAD_DIM = 64
D_SB = SB_HEADS * SB_HEAD_DIM
D_RG = D_MODEL - D_SB
RG_BLOCKS = 8
RG_BLOCK_DIM = D_RG // RG_BLOCKS
CONV_WIDTH = 4
RG_C = 8.0
PEER_HEADS = 8
PEER_N_KEYS = 128
PEER_D_KEY = 256
PEER_HALF = PEER_D_KEY // 2
PEER_TOPK = 16
EPS = 1e-6

VMEM_LIMIT_BYTES = 56 * 1024 * 1024
SUBLANES = 8
LANES = 128

INPROJ_TOKENS = 512
ATT_Q = 512
ATT_K = 256
MIX_TOKENS = 256
PEER_TOKENS = 512
PEER_EXPERTS = 1024
PEER_CHUNK = 256

NEG_INF = float("-inf")
LOG2_E = 1.4426950408889634


def _nt_dot(a, b):
    return lax.dot_general(a, b, (((1,), (1,)), ((), ())), preferred_element_type=F32)


def _softplus(z):
    return jnp.maximum(z, 0.0) + jnp.log(1.0 + jnp.exp(-jnp.abs(z)))


def _gelu_tanh(x):
    c0 = 0.7978845608028654
    c1 = 0.044715 * c0
    half = 0.5 * x
    return half + half * jnp.tanh(x * (c1 * (x * x) + c0))


def _inproj_kernel(x_ref, gmix_ref, wnat_ref, wtq_ref, wtv_ref, qg_ref, kg_ref, bd_ref,
                   qt_ref, k_ref, vt_ref, xrg_ref, grg_ref):
    x = x_ref[...]
    ms = jnp.mean(x * x, axis=-1, keepdims=True)
    h = (x * lax.rsqrt(ms + EPS) * gmix_ref[...]).astype(BF16)

    nat = jnp.dot(h, wnat_ref[...], preferred_element_type=F32)
    k = nat[:, :D_SB]
    kk = k * k
    kk_hi = kk.astype(BF16)
    kk_lo = (kk - kk_hi.astype(F32)).astype(BF16)
    kss = (jnp.dot(kk_hi, bd_ref[...], preferred_element_type=F32)
           + jnp.dot(kk_lo, bd_ref[...], preferred_element_type=F32))
    k_ref[...] = (k * lax.rsqrt(kss * (1.0 / SB_HEAD_DIM) + EPS) * kg_ref[...]).astype(BF16)
    xrg_ref[...] = nat[:, D_SB:D_SB + D_RG]
    grg_ref[...] = nat[:, D_SB + D_RG:]

    t = x.shape[0]
    qt = _nt_dot(wtq_ref[...], h).reshape(SB_HEADS, SB_HEAD_DIM, t)
    qss = jnp.sum(qt * qt, axis=1, keepdims=True)
    qn = qt * lax.rsqrt(qss * (1.0 / SB_HEAD_DIM) + EPS)
    qn = qn.reshape(D_SB, t) * (qg_ref[...] * (SB_HEAD_DIM ** -0.5 * LOG2_E))
    qt_ref[...] = qn.astype(BF16)

    vt = _nt_dot(wtv_ref[...], h).astype(BF16)
    for c in range(t // ATT_K):
        vt_ref[c] = vt[:, c * ATT_K:(c + 1) * ATT_K]


def _inproj(x2d, gmix, wnat, wtq, wtv, qg, kg, bd):
    s = x2d.shape[0]
    t = INPROJ_TOKENS
    n_nat = wnat.shape[1]
    const = lambda shape: pl.BlockSpec(shape, lambda i: (0,) * len(shape))
    return pl.pallas_call(
        _inproj_kernel,
        grid=(s // t,),
        in_specs=[
            pl.BlockSpec((t, D_MODEL), lambda i: (i, 0)),
            const((1, D_MODEL)),
            const((D_MODEL, n_nat)),
            const((D_SB, D_MODEL)),
            const((D_SB, D_MODEL)),
            const((D_SB, 1)),
            const((1, D_SB)),
            const((D_SB, D_SB)),
        ],
        out_specs=[
            pl.BlockSpec((D_SB, t), lambda i: (0, i)),
            pl.BlockSpec((t, D_SB), lambda i: (i, 0)),
            pl.BlockSpec((t // ATT_K, D_SB, ATT_K), lambda i: (i, 0, 0)),
            pl.BlockSpec((t, D_RG), lambda i: (i, 0)),
            pl.BlockSpec((t, D_RG), lambda i: (i, 0)),
        ],
        out_shape=[
            jax.ShapeDtypeStruct((D_SB, s), BF16),
            jax.ShapeDtypeStruct((s, D_SB), BF16),
            jax.ShapeDtypeStruct((s // ATT_K, D_SB, ATT_K), BF16),
            jax.ShapeDtypeStruct((s, D_RG), F32),
            jax.ShapeDtypeStruct((s, D_RG), F32),
        ],
        compiler_params=pltpu.CompilerParams(
            dimension_semantics=("arbitrary",), vmem_limit_bytes=VMEM_LIMIT_BYTES),
        name="inproj",
    )(x2d, gmix, wnat, wtq, wtv, qg, kg, bd)


def _attn_kernel(qt_ref, k_ref, vt_ref, tri_ref, ot_ref, d_ref, s_ref, w_ref):
    hd = pl.program_id(0)
    j = pl.program_id(1)
    q = qt_ref[...]
    zero = jnp.zeros_like(q)
    qpad = jnp.where(hd % 2 == 0,
                     jnp.concatenate([q, zero], axis=0),
                     jnp.concatenate([zero, q], axis=0))
    tri = tri_ref[...]
    subs = ATT_Q // ATT_K
    q0 = j * ATT_Q

    def a_mm(sb):
        out = []
        for c in range(subs):
            ks = pl.multiple_of((sb * subs + c) * ATT_K, ATT_K)
            out.append(jnp.dot(k_ref[pl.ds(ks, ATT_K), :], qpad, preferred_element_type=F32))
        return out

    def a_ew(zs, sb, slot, masked):
        for c, z in enumerate(zs):
            pos = jnp.maximum(z, 0.0)
            neg = jnp.minimum(z, 0.0)
            soft = jnp.log2(1.0 + jnp.exp2(neg - pos))
            sp = pos + soft
            d = neg - soft
            if masked:
                kpos = (sb * subs + c) * ATT_K + lax.broadcasted_iota(jnp.int32, z.shape, 0)
                qpos = q0 + lax.broadcasted_iota(jnp.int32, z.shape, 1)
                valid = kpos < qpos
                sp = jnp.where(valid, sp, 0.0)
                d = jnp.where(valid, d, NEG_INF)
            d_ref[slot, c] = d
            s_ref[slot, c] = sp.astype(BF16)

    def b_mm(slot):
        return [jnp.dot(tri, s_ref[slot, c], preferred_element_type=F32) for c in range(subs)]

    def b_ew(sufs, slot, carry):
        for c in range(subs - 1, -1, -1):
            w_ref[slot, c] = jnp.exp2(d_ref[slot, c] - sufs[c] - carry).astype(BF16)
            carry = carry + (sufs[c][0:1, :] + s_ref[slot, c, 0:1, :].astype(F32))
        return carry

    def c_mm(sb, slot, acc):
        for c in range(subs):
            acc = acc + jnp.dot(vt_ref[sb * subs + c], w_ref[slot, c], preferred_element_type=F32)
        return acc

    def tick(n, slot, carry, acc, do_a=True, do_b=True, do_c=True):
        if do_a:
            zs = a_mm(j - n)
        if do_b:
            sufs = b_mm(1 - slot)
        if do_c:
            acc = c_mm(j - n + 2, slot, acc)
        if do_a:
            a_ew(zs, j - n, slot, False)
        if do_b:
            carry = b_ew(sufs, 1 - slot, carry)
        return carry, acc

    n_steps = j + 1
    carry = jnp.zeros((1, ATT_Q), F32)
    acc = jnp.zeros((SB_HEAD_DIM, ATT_Q), F32)
    a_ew(a_mm(j), j, 0, True)

    def single(ca):
        carry, acc = tick(1, 1, *ca, do_a=False, do_c=False)
        return tick(2, 0, carry, acc, do_a=False, do_b=False)

    def multi(ca):
        carry, acc = tick(1, 1, *ca, do_c=False)

        def body(p, ca):
            carry, acc = tick(2 * p + 2, 0, *ca)
            return tick(2 * p + 3, 1, carry, acc)

        carry, acc = lax.fori_loop(0, (n_steps - 2) // 2, body, (carry, acc))

        def tail_even(ca):
            carry, acc = tick(n_steps, 0, *ca, do_a=False)
            return tick(n_steps + 1, 1, carry, acc, do_a=False, do_b=False)

        def tail_odd(ca):
            carry, acc = tick(n_steps - 1, 0, *ca)
            carry, acc = tick(n_steps, 1, carry, acc, do_a=False)
            return tick(n_steps + 1, 0, carry, acc, do_a=False, do_b=False)

        return lax.cond(n_steps % 2 == 0, tail_even, tail_odd, (carry, acc))

    carry, acc = lax.cond(n_steps == 1, single, multi, (carry, acc))
    ot_ref[...] = acc


def _attention(qt, k, vt, tri):
    s = k.shape[0]
    return pl.pallas_call(
        _attn_kernel,
        grid=(SB_HEADS, s // ATT_Q),
        in_specs=[
            pl.BlockSpec((SB_HEAD_DIM, ATT_Q), lambda h, j: (h, j)),
            pl.BlockSpec((s, 2 * SB_HEAD_DIM), lambda h, j: (0, h // 2)),
            pl.BlockSpec((s // ATT_K, SB_HEAD_DIM, ATT_K), lambda h, j: (0, h, 0)),
            pl.BlockSpec((ATT_K, ATT_K), lambda h, j: (0, 0)),
        ],
        out_specs=pl.BlockSpec((SB_HEAD_DIM, ATT_Q), lambda h, j: (h, j)),
        out_shape=jax.ShapeDtypeStruct((D_SB, s), F32),
        scratch_shapes=[
            pltpu.VMEM((2, ATT_Q // ATT_K, ATT_K, ATT_Q), F32),
            pltpu.VMEM((2, ATT_Q // ATT_K, ATT_K, ATT_Q), BF16),
            pltpu.VMEM((2, ATT_Q // ATT_K, ATT_K, ATT_Q), BF16),
        ],
        compiler_params=pltpu.CompilerParams(
            dimension_semantics=("arbitrary", "arbitrary"), vmem_limit_bytes=VMEM_LIMIT_BYTES),
        name="sb_attention",
    )(qt, k, vt, tri)


def _mix_kernel(xrg_ref, grg_ref, ot_ref, x_ref, cw_ref, cb_ref, wa_ref, ba_ref, wx_ref, bx_ref,
                lam_ref, nsb_ref, nrg_ref, wout_ref, nffn_ref,
                x2_ref, h2t_ref, xs_ref, hs_ref, hstate_ref):
    t = xrg_ref.shape[0]
    hist = SUBLANES

    @pl.when(pl.program_id(0) == 0)
    def _():
        xs_ref[0:hist, :] = jnp.zeros((hist, D_RG), F32)
        hstate_ref[...] = jnp.zeros_like(hstate_ref)

    xs_ref[hist:hist + t, :] = xrg_ref[...]
    y = cb_ref[...] + cw_ref[0:1, :] * xs_ref[hist - 3:hist - 3 + t, :]
    for jj in range(1, CONV_WIDTH):
        y = y + cw_ref[jj:jj + 1, :] * xs_ref[hist - 3 + jj:hist - 3 + jj + t, :]
    xs_ref[0:hist, :] = xs_ref[t:t + hist, :]

    yb = y.astype(BF16)
    r = jax.nn.sigmoid(jnp.dot(yb, wa_ref[...], preferred_element_type=F32) + ba_ref[...])
    ig = jax.nn.sigmoid(jnp.dot(yb, wx_ref[...], preferred_element_type=F32) + bx_ref[...])
    log_a = (-RG_C) * r * _softplus(-lam_ref[...])
    a = jnp.exp(log_a)
    b = jnp.sqrt(jnp.tanh(-log_a) * (1.0 + a * a)) * (ig * y)

    rowmod = lax.broadcasted_iota(jnp.int32, (t, D_RG), 0) % SUBLANES
    d = 1
    while d < SUBLANES:
        keep = rowmod >= d
        a_sh = pltpu.roll(a, d, axis=0)
        b_sh = pltpu.roll(b, d, axis=0)
        b = jnp.where(keep, b + a * b_sh, b)
        a = jnp.where(keep, a * a_sh, a)
        d *= 2
    hprev = hstate_ref[...]
    for g in range(t // SUBLANES):
        sl = slice(g * SUBLANES, (g + 1) * SUBLANES)
        hg = a[sl] * hprev + b[sl]
        hs_ref[sl, :] = hg
        hprev = jnp.broadcast_to(hg[SUBLANES - 1:SUBLANES, :], (SUBLANES, D_RG))
    hstate_ref[...] = hprev

    o_rg = hs_ref[...] * _gelu_tanh(grg_ref[...])
    o_sb = ot_ref[...].T

    def rms(v, g):
        return v * lax.rsqrt(jnp.mean(v * v, axis=-1, keepdims=True) + EPS) * g

    n_sb = rms(o_sb, nsb_ref[...]).astype(BF16)
    n_rg = rms(o_rg, nrg_ref[...]).astype(BF16)
    mix = (jnp.dot(n_sb, wout_ref[0:D_SB, :], preferred_element_type=F32)
           + jnp.dot(n_rg, wout_ref[D_SB:, :], preferred_element_type=F32))
    x2 = x_ref[...] + mix
    x2_ref[...] = x2
    h2t_ref[...] = rms(x2, nffn_ref[...]).T.astype(BF16)


def _mix(xrg, grg, ot, x2d, cw, cb, wa, ba, wx, bx, lam, nsb, nrg, wout, nffn):
    s = x2d.shape[0]
    t = MIX_TOKENS
    const = lambda shape: pl.BlockSpec(shape, lambda i: (0,) * len(shape))
    return pl.pallas_call(
        _mix_kernel,
        grid=(s // t,),
        in_specs=[
            pl.BlockSpec((t, D_RG), lambda i: (i, 0)),
            pl.BlockSpec((t, D_RG), lambda i: (i, 0)),
            pl.BlockSpec((D_SB, t), lambda i: (0, i)),
            pl.BlockSpec((t, D_MODEL), lambda i: (i, 0)),
            const((CONV_WIDTH, D_RG)), const((1, D_RG)),
            const((D_RG, D_RG)), const((1, D_RG)),
            const((D_RG, D_RG)), const((1, D_RG)),
            const((1, D_RG)), const((1, D_SB)), const((1, D_RG)),
            const((D_MODEL, D_MODEL)), const((1, D_MODEL)),
        ],
        out_specs=[
            pl.BlockSpec((t, D_MODEL), lambda i: (i, 0)),
            pl.BlockSpec((D_MODEL, t), lambda i: (0, i)),
        ],
        out_shape=[
            jax.ShapeDtypeStruct((s, D_MODEL), F32),
            jax.ShapeDtypeStruct((D_MODEL, s), BF16),
        ],
        scratch_shapes=[
            pltpu.VMEM((t + SUBLANES, D_RG), F32),
            pltpu.VMEM((t, D_RG), F32),
            pltpu.VMEM((SUBLANES, D_RG), F32),
        ],
        compiler_params=pltpu.CompilerParams(
            dimension_semantics=("arbitrary",), vmem_limit_bytes=VMEM_LIMIT_BYTES),
        name="rglru_outproj",
    )(xrg, grg, ot, x2d, cw, cb, wa, ba, wx, bx, lam, nsb, nrg, wout, nffn)


_N_RANK = PEER_TOPK + 1
_CAND_PAIRS = [(a, b) for a in range(_N_RANK) for b in range(_N_RANK)
               if (a + 1) * (b + 1) <= _N_RANK]


def _top_desc(s, n, with_rank=False):
    out = []
    rank = jnp.full(s.shape, float(s.shape[0]), F32) if with_rank else None
    for r in range(n):
        m = jnp.max(s, axis=0, keepdims=True)
        out.append(m)
        hit = s == m
        if with_rank and r < n - 1:
            rank = jnp.where(hit, float(r), rank)
        s = jnp.where(hit, NEG_INF, s)
    return (out, rank) if with_rank else out


def _peer_route(ht, wq_ref, keys_ref, e1_ref, n1_ref, e2_ref, r2_ref):
    for hd in range(PEER_HEADS):
        qt = jnp.dot(wq_ref[hd * PEER_D_KEY:(hd + 1) * PEER_D_KEY, :], ht,
                     preferred_element_type=F32).astype(BF16)
        s1 = jnp.dot(keys_ref[hd, 0], qt[:PEER_HALF], preferred_element_type=F32)
        s2 = jnp.dot(keys_ref[hd, 1], qt[PEER_HALF:], preferred_element_type=F32)
        top1 = _top_desc(s1, _N_RANK)
        top2, rank2 = _top_desc(s2, _N_RANK, with_rank=True)
        cand = jnp.concatenate([top1[a] + top2[b] for a, b in _CAND_PAIRS], axis=0)
        best = _top_desc(cand, _N_RANK)
        zsum = jnp.ones_like(best[0])
        for kk in range(1, PEER_TOPK):
            zsum = zsum + jnp.exp(best[kk] - best[0])
        tau = 0.5 * (best[PEER_TOPK - 1] + best[PEER_TOPK])
        theta = tau - s1
        count1 = jnp.zeros_like(s1)
        for r in range(PEER_TOPK):
            count1 = count1 + jnp.where(top2[r] > theta, 1.0, 0.0)
        e1_ref[hd] = jnp.exp(s1 - top1[0]) / zsum
        n1_ref[hd] = count1
        e2_ref[hd] = jnp.exp(s2 - top2[0]).astype(BF16)
        r2_ref[hd] = rank2.astype(BF16)


def _peer_kernel(ht_ref, x2_ref, wq_ref, keys_ref, u_ref, vt_ref, out_ref,
                 e1_ref, n1_ref, e2_ref, r2_ref, p_ref, acc_ref):
    e = pl.program_id(1)
    t = ht_ref.shape[1]
    n1 = PEER_EXPERTS // PEER_N_KEYS
    assert n1 == SUBLANES

    @pl.when(e == 0)
    def _():
        _peer_route(ht_ref[...], wq_ref, keys_ref, e1_ref, n1_ref, e2_ref, r2_ref)
        acc_ref[...] = jnp.zeros_like(acc_ref)

    i1_base = pl.multiple_of(e * n1, SUBLANES)
    ht = ht_ref[...]
    il_per_chunk = PEER_CHUNK // PEER_N_KEYS
    rows = 2 * SUBLANES
    n_row_tiles = PEER_N_KEYS // rows
    n_chunks = PEER_EXPERTS // PEER_CHUNK
    chunk = lambda ch: slice(ch * PEER_CHUNK, (ch + 1) * PEER_CHUNK)
    pre_all = jnp.dot(u_ref[...], ht, preferred_element_type=F32)
    pre = [pre_all[chunk(ch)] for ch in range(n_chunks)]
    for ch in range(n_chunks):
        for tc in range(t // LANES):
            ls = slice(tc * LANES, (tc + 1) * LANES)
            w = [[jnp.zeros((rows, LANES), BF16) for _ in range(n_row_tiles)]
                 for _ in range(il_per_chunk)]
            for hd in range(PEER_HEADS):
                cnt8 = n1_ref[hd, pl.ds(i1_base, n1), ls]
                e18 = e1_ref[hd, pl.ds(i1_base, n1), ls]
                cnt, e1 = [], []
                for ii in range(il_per_chunk):
                    il = ch * il_per_chunk + ii
                    cnt.append(jnp.broadcast_to(cnt8[il:il + 1], (rows, LANES)).astype(BF16))
                    e1.append(jnp.broadcast_to(e18[il:il + 1], (rows, LANES)).astype(BF16))
                for rt in range(n_row_tiles):
                    rs = slice(rt * rows, (rt + 1) * rows)
                    r2 = r2_ref[hd, rs, ls]
                    e2 = e2_ref[hd, rs, ls]
                    for ii in range(il_per_chunk):
                        w[ii][rt] = w[ii][rt] + jnp.where(r2 < cnt[ii], e1[ii] * e2, 0.0)
            act = _gelu_tanh(pre[ch][:, ls]).astype(BF16)
            for ii in range(il_per_chunk):
                for rt in range(n_row_tiles):
                    r0 = ii * PEER_N_KEYS + rt * rows
                    p_ref[ch * PEER_CHUNK + r0:ch * PEER_CHUNK + r0 + rows, ls] = (
                        w[ii][rt] * act[r0:r0 + rows])
    acc_ref[...] += jnp.dot(vt_ref[...], p_ref[...], preferred_element_type=F32)

    @pl.when(e == pl.num_programs(1) - 1)
    def _():
        out_ref[...] = x2_ref[...] + acc_ref[...].T


def _peer(h2t, x2, wq_t, keys, u_bf, vt_bf):
    s = h2t.shape[1]
    t = PEER_TOKENS
    n_exp = u_bf.shape[0]
    et = PEER_EXPERTS
    rt = lambda dtype: pltpu.VMEM((PEER_HEADS, PEER_N_KEYS, t), dtype)
    return pl.pallas_call(
        _peer_kernel,
        grid=(s // t, n_exp // et),
        in_specs=[
            pl.BlockSpec((D_MODEL, t), lambda i, e: (0, i)),
            pl.BlockSpec((t, D_MODEL), lambda i, e: (i, 0)),
            pl.BlockSpec((PEER_HEADS * PEER_D_KEY, D_MODEL), lambda i, e: (0, 0)),
            pl.BlockSpec((PEER_HEADS, 2, PEER_N_KEYS, PEER_HALF), lambda i, e: (0, 0, 0, 0)),
            pl.BlockSpec((et, D_MODEL), lambda i, e: (e, 0)),
            pl.BlockSpec((D_MODEL, et), lambda i, e: (0, e)),
        ],
        out_specs=pl.BlockSpec((t, D_MODEL), lambda i, e: (i, 0)),
        out_shape=jax.ShapeDtypeStruct((s, D_MODEL), F32),
        scratch_shapes=[rt(F32), rt(F32), rt(BF16), rt(BF16),
                        pltpu.VMEM((et, t), BF16),
                        pltpu.VMEM((D_MODEL, t), F32)],
        compiler_params=pltpu.CompilerParams(
            dimension_semantics=("arbitrary", "arbitrary"), vmem_limit_bytes=VMEM_LIMIT_BYTES),
        name="peer",
    )(h2t, x2, wq_t, keys, u_bf, vt_bf)


def _block_diag(w):
    n, bi, bj = w.shape
    eye = jnp.eye(n, dtype=w.dtype)
    return (eye[:, None, :, None] * w[:, :, None, :]).reshape(n * bi, n * bj)


def kernel(x, norm_mix, w_in, q_norm, k_norm, conv_w, conv_b, rg_w_a, rg_b_a, rg_w_x, rg_b_x,
           rg_lambda, out_norm_sb, out_norm_rg, w_out, norm_ffn, peer_w_query, peer_sub_keys,
           peer_u, peer_v):
    bsz, s, d = x.shape
    assert bsz == 1 and d == D_MODEL
    assert s % max(INPROJ_TOKENS, ATT_Q, MIX_TOKENS, PEER_TOKENS) == 0
    depth = w_in.shape[0]
    x2d = x.reshape(s, d)

    head_id = jnp.arange(D_SB) // SB_HEAD_DIM
    bd = (head_id[:, None] == head_id[None, :]).astype(BF16)
    kidx = jnp.arange(ATT_K)
    tri = (kidx[None, :] > kidx[:, None]).astype(BF16)
    row = lambda v: v.reshape(1, -1)

    for l in range(depth):
        w = w_in[l]
        wtq = w[:, 0:D_SB].T.astype(BF16)
        wtv = w[:, 2 * D_SB:3 * D_SB].T.astype(BF16)
        wnat = jnp.concatenate([w[:, D_SB:2 * D_SB], w[:, 3 * D_SB:]], axis=1).astype(BF16)
        qg = jnp.tile(q_norm[l], SB_HEADS).reshape(D_SB, 1)
        kg = jnp.tile(k_norm[l], SB_HEADS).reshape(1, D_SB)
        qt, k, vt, xrg, grg = _inproj(x2d, row(norm_mix[l]), wnat, wtq, wtv, qg, kg, bd)

        ot = _attention(qt, k, vt, tri)

        x2, h2t = _mix(xrg, grg, ot, x2d, conv_w[l], row(conv_b[l]),
                      _block_diag(rg_w_a[l]).astype(BF16), row(rg_b_a[l]),
                      _block_diag(rg_w_x[l]).astype(BF16), row(rg_b_x[l]),
                      row(rg_lambda[l]), row(out_norm_sb[l]), row(out_norm_rg[l]),
                      w_out[l].astype(BF16), row(norm_ffn[l]))

        wq_t = peer_w_query[l].reshape(d, PEER_HEADS * PEER_D_KEY).T.astype(BF16)
        x2d = _peer(h2t, x2, wq_t, peer_sub_keys[l].astype(BF16),
                    peer_u[l].astype(BF16), peer_v[l].T.astype(BF16))
    return x2d.reshape(bsz, s, d)
```

```python
import functools

import jax
import jax.numpy as jnp
from jax import lax
from jax.experimental import pallas as pl
from jax.experimental.pallas import tpu as pltpu

F32 = jnp.float32
BF16 = jnp.bfloat16

D_MODEL = 1024
SB_HEADS = 8
SB_HEAD_DIM = 64
D_SB = SB_HEADS * SB_HEAD_DIM
D_RG = D_MODEL - D_SB
RG_BLOCKS = 8
RG_BLOCK_DIM = D_RG // RG_BLOCKS
CONV_WIDTH = 4
RG_C = 8.0
PEER_HEADS = 8
PEER_N_KEYS = 128
PEER_D_KEY = 256
PEER_HALF = PEER_D_KEY // 2
PEER_TOPK = 16
EPS = 1e-6

VMEM_LIMIT_BYTES = 56 * 1024 * 1024
SUBLANES = 8
LANES = 128

INPROJ_TOKENS = 512
ATT_Q = 512
ATT_K = 256
MIX_TOKENS = 256
PEER_TOKENS = 512
PEER_EXPERTS = 1024
PEER_CHUNK = 256

NEG_INF = float("-inf")


def _nt_dot(a, b):
    return lax.dot_general(a, b, (((1,), (1,)), ((), ())), preferred_element_type=F32)


def _softplus(z):
    return jnp.maximum(z, 0.0) + jnp.log(1.0 + jnp.exp(-jnp.abs(z)))


def _gelu_tanh(x):
    c0 = 0.7978845608028654
    c1 = 0.044715 * c0
    half = 0.5 * x
    return half + half * jnp.tanh(x * (c1 * (x * x) + c0))


def _inproj_kernel(x_ref, gmix_ref, wnat_ref, wtq_ref, wtv_ref, qg_ref, kg_ref, bd_ref,
                   qt_ref, k_ref, vt_ref, xrg_ref, grg_ref):
    x = x_ref[...]
    ms = jnp.mean(x * x, axis=-1, keepdims=True)
    h = (x * lax.rsqrt(ms + EPS) * gmix_ref[...]).astype(BF16)

    nat = jnp.dot(h, wnat_ref[...], preferred_element_type=F32)
    k = nat[:, :D_SB]
    kk = k * k
    kk_hi = kk.astype(BF16)
    kk_lo = (kk - kk_hi.astype(F32)).astype(BF16)
    kss = (jnp.dot(kk_hi, bd_ref[...], preferred_element_type=F32)
           + jnp.dot(kk_lo, bd_ref[...], preferred_element_type=F32))
    k_ref[...] = (k * lax.rsqrt(kss * (1.0 / SB_HEAD_DIM) + EPS) * kg_ref[...]).astype(BF16)
    xrg_ref[...] = nat[:, D_SB:D_SB + D_RG]
    grg_ref[...] = nat[:, D_SB + D_RG:]

    t = x.shape[0]
    qt = _nt_dot(wtq_ref[...], h).reshape(SB_HEADS, SB_HEAD_DIM, t)
    qss = jnp.sum(qt * qt, axis=1, keepdims=True)
    qn = qt * lax.rsqrt(qss * (1.0 / SB_HEAD_DIM) + EPS)
    qn = qn.reshape(D_SB, t) * (qg_ref[...] * (SB_HEAD_DIM ** -0.5))
    qt_ref[...] = qn.astype(BF16)

    vt = _nt_dot(wtv_ref[...], h).astype(BF16)
    for c in range(t // ATT_K):
        vt_ref[c] = vt[:, c * ATT_K:(c + 1) * ATT_K]


def _inproj(x2d, gmix, wnat, wtq, wtv, qg, kg, bd):
    s = x2d.shape[0]
    t = INPROJ_TOKENS
    n_nat = wnat.shape[1]
    const = lambda shape: pl.BlockSpec(shape, lambda i: (0,) * len(shape))
    return pl.pallas_call(
        _inproj_kernel,
        grid=(s // t,),
        in_specs=[
            pl.BlockSpec((t, D_MODEL), lambda i: (i, 0)),
            const((1, D_MODEL)),
            const((D_MODEL, n_nat)),
            const((D_SB, D_MODEL)),
            const((D_SB, D_MODEL)),
            const((D_SB, 1)),
            const((1, D_SB)),
            const((D_SB, D_SB)),
        ],
        out_specs=[
            pl.BlockSpec((D_SB, t), lambda i: (0, i)),
            pl.BlockSpec((t, D_SB), lambda i: (i, 0)),
            pl.BlockSpec((t // ATT_K, D_SB, ATT_K), lambda i: (i, 0, 0)),
            pl.BlockSpec((t, D_RG), lambda i: (i, 0)),
            pl.BlockSpec((t, D_RG), lambda i: (i, 0)),
        ],
        out_shape=[
            jax.ShapeDtypeStruct((D_SB, s), BF16),
            jax.ShapeDtypeStruct((s, D_SB), BF16),
            jax.ShapeDtypeStruct((s // ATT_K, D_SB, ATT_K), BF16),
            jax.ShapeDtypeStruct((s, D_RG), F32),
            jax.ShapeDtypeStruct((s, D_RG), F32),
        ],
        compiler_params=pltpu.CompilerParams(
            dimension_semantics=("arbitrary",), vmem_limit_bytes=VMEM_LIMIT_BYTES),
        name="inproj",
    )(x2d, gmix, wnat, wtq, wtv, qg, kg, bd)


def _attn_kernel(qt_ref, k_ref, vt_ref, tri_ref, ot_ref, d_ref, s_ref, w_ref):
    hd = pl.program_id(0)
    j = pl.program_id(1)
    q = qt_ref[...]
    zero = jnp.zeros_like(q)
    qpad = jnp.where(hd % 2 == 0,
                     jnp.concatenate([q, zero], axis=0),
                     jnp.concatenate([zero, q], axis=0))
    tri = tri_ref[...]
    subs = ATT_Q // ATT_K
    q0 = j * ATT_Q

    def a_mm(sb):
        out = []
        for c in range(subs):
            ks = pl.multiple_of((sb * subs + c) * ATT_K, ATT_K)
            out.append(jnp.dot(k_ref[pl.ds(ks, ATT_K), :], qpad, preferred_element_type=F32))
        return out

    def a_ew(zs, sb, slot, masked):
        for c, z in enumerate(zs):
            sp = jnp.maximum(z, 0.0) + jnp.log(1.0 + jnp.exp(-jnp.abs(z)))
            d = z - sp
            if masked:
                kpos = (sb * subs + c) * ATT_K + lax.broadcasted_iota(jnp.int32, z.shape, 0)
                qpos = q0 + lax.broadcasted_iota(jnp.int32, z.shape, 1)
                valid = kpos < qpos
                sp = jnp.where(valid, sp, 0.0)
                d = jnp.where(valid, d, NEG_INF)
            d_ref[slot, c] = d
            s_ref[slot, c] = sp.astype(BF16)

    def b_mm(slot):
        return [jnp.dot(tri, s_ref[slot, c], preferred_element_type=F32) for c in range(subs)]

    def b_ew(sufs, slot, carry):
        for c in range(subs - 1, -1, -1):
            w_ref[slot, c] = jnp.exp(d_ref[slot, c] - sufs[c] - carry).astype(BF16)
            carry = carry + (sufs[c][0:1, :] + s_ref[slot, c, 0:1, :].astype(F32))
        return carry

    def c_mm(sb, slot, acc):
        for c in range(subs):
            acc = acc + jnp.dot(vt_ref[sb * subs + c], w_ref[slot, c], preferred_element_type=F32)
        return acc

    def tick(n, slot, carry, acc, do_a=True, do_b=True, do_c=True):
        if do_a:
            zs = a_mm(j - n)
        if do_b:
            sufs = b_mm(1 - slot)
        if do_c:
            acc = c_mm(j - n + 2, slot, acc)
        if do_a:
            a_ew(zs, j - n, slot, False)
        if do_b:
            carry = b_ew(sufs, 1 - slot, carry)
        return carry, acc

    n_steps = j + 1
    carry = jnp.zeros((1, ATT_Q), F32)
    acc = jnp.zeros((SB_HEAD_DIM, ATT_Q), F32)
    a_ew(a_mm(j), j, 0, True)

    def single(ca):
        carry, acc = tick(1, 1, *ca, do_a=False, do_c=False)
        return tick(2, 0, carry, acc, do_a=False, do_b=False)

    def multi(ca):
        carry, acc = tick(1, 1, *ca, do_c=False)

        def body(p, ca):
            carry, acc = tick(2 * p + 2, 0, *ca)
            return tick(2 * p + 3, 1, carry, acc)

        carry, acc = lax.fori_loop(0, (n_steps - 2) // 2, body, (carry, acc))

        def tail_even(ca):
            carry, acc = tick(n_steps, 0, *ca, do_a=False)
            return tick(n_steps + 1, 1, carry, acc, do_a=False, do_b=False)

        def tail_odd(ca):
            carry, acc = tick(n_steps - 1, 0, *ca)
            carry, acc = tick(n_steps, 1, carry, acc, do_a=False)
            return tick(n_steps + 1, 0, carry, acc, do_a=False, do_b=False)

        return lax.cond(n_steps % 2 == 0, tail_even, tail_odd, (carry, acc))

    carry, acc = lax.cond(n_steps == 1, single, multi, (carry, acc))
    ot_ref[...] = acc


def _attention(qt, k, vt, tri):
    s = k.shape[0]
    return pl.pallas_call(
        _attn_kernel,
        grid=(SB_HEADS, s // ATT_Q),
        in_specs=[
            pl.BlockSpec((SB_HEAD_DIM, ATT_Q), lambda h, j: (h, j)),
            pl.BlockSpec((s, 2 * SB_HEAD_DIM), lambda h, j: (0, h // 2)),
            pl.BlockSpec((s // ATT_K, SB_HEAD_DIM, ATT_K), lambda h, j: (0, h, 0)),
            pl.BlockSpec((ATT_K, ATT_K), lambda h, j: (0, 0)),
        ],
        out_specs=pl.BlockSpec((SB_HEAD_DIM, ATT_Q), lambda h, j: (h, j)),
        out_shape=jax.ShapeDtypeStruct((D_SB, s), F32),
        scratch_shapes=[
            pltpu.VMEM((2, ATT_Q // ATT_K, ATT_K, ATT_Q), F32),
            pltpu.VMEM((2, ATT_Q // ATT_K, ATT_K, ATT_Q), BF16),
            pltpu.VMEM((2, ATT_Q // ATT_K, ATT_K, ATT_Q), BF16),
        ],
        compiler_params=pltpu.CompilerParams(
            dimension_semantics=("arbitrary", "arbitrary"), vmem_limit_bytes=VMEM_LIMIT_BYTES),
        name="sb_attention",
    )(qt, k, vt, tri)


def _mix_kernel(xrg_ref, grg_ref, ot_ref, x_ref, cw_ref, cb_ref, wa_ref, ba_ref, wx_ref, bx_ref,
                lam_ref, nsb_ref, nrg_ref, wout_ref, nffn_ref,
                x2_ref, h2t_ref, xs_ref, hs_ref, hstate_ref):
    t = xrg_ref.shape[0]
    hist = SUBLANES

    @pl.when(pl.program_id(0) == 0)
    def _():
        xs_ref[0:hist, :] = jnp.zeros((hist, D_RG), F32)
        hstate_ref[...] = jnp.zeros_like(hstate_ref)

    xs_ref[hist:hist + t, :] = xrg_ref[...]
    y = cb_ref[...] + cw_ref[0:1, :] * xs_ref[hist - 3:hist - 3 + t, :]
    for jj in range(1, CONV_WIDTH):
        y = y + cw_ref[jj:jj + 1, :] * xs_ref[hist - 3 + jj:hist - 3 + jj + t, :]
    xs_ref[0:hist, :] = xs_ref[t:t + hist, :]

    yb = y.astype(BF16)
    r = jax.nn.sigmoid(jnp.dot(yb, wa_ref[...], preferred_element_type=F32) + ba_ref[...])
    ig = jax.nn.sigmoid(jnp.dot(yb, wx_ref[...], preferred_element_type=F32) + bx_ref[...])
    log_a = (-RG_C) * r * _softplus(-lam_ref[...])
    a = jnp.exp(log_a)
    b = jnp.sqrt(jnp.tanh(-log_a) * (1.0 + a * a)) * (ig * y)

    rowmod = lax.broadcasted_iota(jnp.int32, (t, D_RG), 0) % SUBLANES
    d = 1
    while d < SUBLANES:
        keep = rowmod >= d
        a_sh = pltpu.roll(a, d, axis=0)
        b_sh = pltpu.roll(b, d, axis=0)
        b = jnp.where(keep, b + a * b_sh, b)
        a = jnp.where(keep, a * a_sh, a)
        d *= 2
    hprev = hstate_ref[...]
    for g in range(t // SUBLANES):
        sl = slice(g * SUBLANES, (g + 1) * SUBLANES)
        hg = a[sl] * hprev + b[sl]
        hs_ref[sl, :] = hg
        hprev = jnp.broadcast_to(hg[SUBLANES - 1:SUBLANES, :], (SUBLANES, D_RG))
    hstate_ref[...] = hprev

    o_rg = hs_ref[...] * _gelu_tanh(grg_ref[...])
    o_sb = ot_ref[...].T

    def rms(v, g):
        return v * lax.rsqrt(jnp.mean(v * v, axis=-1, keepdims=True) + EPS) * g

    n_sb = rms(o_sb, nsb_ref[...]).astype(BF16)
    n_rg = rms(o_rg, nrg_ref[...]).astype(BF16)
    mix = (jnp.dot(n_sb, wout_ref[0:D_SB, :], preferred_element_type=F32)
           + jnp.dot(n_rg, wout_ref[D_SB:, :], preferred_element_type=F32))
    x2 = x_ref[...] + mix
    x2_ref[...] = x2
    h2t_ref[...] = rms(x2, nffn_ref[...]).T.astype(BF16)


def _mix(xrg, grg, ot, x2d, cw, cb, wa, ba, wx, bx, lam, nsb, nrg, wout, nffn):
    s = x2d.shape[0]
    t = MIX_TOKENS
    const = lambda shape: pl.BlockSpec(shape, lambda i: (0,) * len(shape))
    return pl.pallas_call(
        _mix_kernel,
        grid=(s // t,),
        in_specs=[
            pl.BlockSpec((t, D_RG), lambda i: (i, 0)),
            pl.BlockSpec((t, D_RG), lambda i: (i, 0)),
            pl.BlockSpec((D_SB, t), lambda i: (0, i)),
            pl.BlockSpec((t, D_MODEL), lambda i: (i, 0)),
            const((CONV_WIDTH, D_RG)), const((1, D_RG)),
            const((D_RG, D_RG)), const((1, D_RG)),
            const((D_RG, D_RG)), const((1, D_RG)),
            const((1, D_RG)), const((1, D_SB)), const((1, D_RG)),
            const((D_MODEL, D_MODEL)), const((1, D_MODEL)),
        ],
        out_specs=[
            pl.BlockSpec((t, D_MODEL), lambda i: (i, 0)),
            pl.BlockSpec((D_MODEL, t), lambda i: (0, i)),
        ],
        out_shape=[
            jax.ShapeDtypeStruct((s, D_MODEL), F32),
            jax.ShapeDtypeStruct((D_MODEL, s), BF16),
        ],
        scratch_shapes=[
            pltpu.VMEM((t + SUBLANES, D_RG), F32),
            pltpu.VMEM((t, D_RG), F32),
            pltpu.VMEM((SUBLANES, D_RG), F32),
        ],
        compiler_params=pltpu.CompilerParams(
            dimension_semantics=("arbitrary",), vmem_limit_bytes=VMEM_LIMIT_BYTES),
        name="rglru_outproj",
    )(xrg, grg, ot, x2d, cw, cb, wa, ba, wx, bx, lam, nsb, nrg, wout, nffn)


_N_RANK = PEER_TOPK + 1
_CAND_PAIRS = [(a, b) for a in range(_N_RANK) for b in range(_N_RANK)
               if (a + 1) * (b + 1) <= _N_RANK]


def _top_desc(s, n, with_rank=False):
    out = []
    rank = jnp.full(s.shape, float(s.shape[0]), F32) if with_rank else None
    for r in range(n):
        m = jnp.max(s, axis=0, keepdims=True)
        out.append(m)
        hit = s == m
        if with_rank and r < n - 1:
            rank = jnp.where(hit, float(r), rank)
        s = jnp.where(hit, NEG_INF, s)
    return (out, rank) if with_rank else out


def _peer_route(ht, wq_ref, keys_ref, e1_ref, n1_ref, e2_ref, r2_ref):
    for hd in range(PEER_HEADS):
        qt = jnp.dot(wq_ref[hd * PEER_D_KEY:(hd + 1) * PEER_D_KEY, :], ht,
                     preferred_element_type=F32).astype(BF16)
        s1 = jnp.dot(keys_ref[hd, 0], qt[:PEER_HALF], preferred_element_type=F32)
        s2 = jnp.dot(keys_ref[hd, 1], qt[PEER_HALF:], preferred_element_type=F32)
        top1 = _top_desc(s1, _N_RANK)
        top2, rank2 = _top_desc(s2, _N_RANK, with_rank=True)
        cand = jnp.concatenate([top1[a] + top2[b] for a, b in _CAND_PAIRS], axis=0)
        best = _top_desc(cand, _N_RANK)
        zsum = jnp.ones_like(best[0])
        for kk in range(1, PEER_TOPK):
            zsum = zsum + jnp.exp(best[kk] - best[0])
        tau = 0.5 * (best[PEER_TOPK - 1] + best[PEER_TOPK])
        theta = tau - s1
        count1 = jnp.zeros_like(s1)
        for r in range(PEER_TOPK):
            count1 = count1 + jnp.where(top2[r] > theta, 1.0, 0.0)
        e1_ref[hd] = jnp.exp(s1 - top1[0]) / zsum
        n1_ref[hd] = count1
        e2_ref[hd] = jnp.exp(s2 - top2[0]).astype(BF16)
        r2_ref[hd] = rank2.astype(BF16)


def _peer_kernel(ht_ref, x2_ref, wq_ref, keys_ref, u_ref, vt_ref, out_ref,
                 e1_ref, n1_ref, e2_ref, r2_ref, p_ref, acc_ref):
    e = pl.program_id(1)
    t = ht_ref.shape[1]
    n1 = PEER_EXPERTS // PEER_N_KEYS
    assert n1 == SUBLANES

    @pl.when(e == 0)
    def _():
        _peer_route(ht_ref[...], wq_ref, keys_ref, e1_ref, n1_ref, e2_ref, r2_ref)
        acc_ref[...] = jnp.zeros_like(acc_ref)

    i1_base = pl.multiple_of(e * n1, SUBLANES)
    ht = ht_ref[...]
    il_per_chunk = PEER_CHUNK // PEER_N_KEYS
    rows = 2 * SUBLANES
    n_row_tiles = PEER_N_KEYS // rows
    n_chunks = PEER_EXPERTS // PEER_CHUNK
    chunk = lambda ch: slice(ch * PEER_CHUNK, (ch + 1) * PEER_CHUNK)
    pre_all = jnp.dot(u_ref[...], ht, preferred_element_type=F32)
    pre = [pre_all[chunk(ch)] for ch in range(n_chunks)]
    for ch in range(n_chunks):
        for tc in range(t // LANES):
            ls = slice(tc * LANES, (tc + 1) * LANES)
            w = [[jnp.zeros((rows, LANES), BF16) for _ in range(n_row_tiles)]
                 for _ in range(il_per_chunk)]
            for hd in range(PEER_HEADS):
                cnt8 = n1_ref[hd, pl.ds(i1_base, n1), ls]
                e18 = e1_ref[hd, pl.ds(i1_base, n1), ls]
                cnt, e1 = [], []
                for ii in range(il_per_chunk):
                    il = ch * il_per_chunk + ii
                    cnt.append(jnp.broadcast_to(cnt8[il:il + 1], (rows, LANES)).astype(BF16))
                    e1.append(jnp.broadcast_to(e18[il:il + 1], (rows, LANES)).astype(BF16))
                for rt in range(n_row_tiles):
                    rs = slice(rt * rows, (rt + 1) * rows)
                    r2 = r2_ref[hd, rs, ls]
                    e2 = e2_ref[hd, rs, ls]
                    for ii in range(il_per_chunk):
                        w[ii][rt] = w[ii][rt] + jnp.where(r2 < cnt[ii], e1[ii] * e2, 0.0)
            act = _gelu_tanh(pre[ch][:, ls]).astype(BF16)
            for ii in range(il_per_chunk):
                for rt in range(n_row_tiles):
                    r0 = ii * PEER_N_KEYS + rt * rows
                    p_ref[ch * PEER_CHUNK + r0:ch * PEER_CHUNK + r0 + rows, ls] = (
                        w[ii][rt] * act[r0:r0 + rows])
    acc_ref[...] += jnp.dot(vt_ref[...], p_ref[...], preferred_element_type=F32)

    @pl.when(e == pl.num_programs(1) - 1)
    def _():
        out_ref[...] = x2_ref[...] + acc_ref[...].T


def _peer(h2t, x2, wq_t, keys, u_bf, vt_bf):
    s = h2t.shape[1]
    t = PEER_TOKENS
    n_exp = u_bf.shape[0]
    et = PEER_EXPERTS
    rt = lambda dtype: pltpu.VMEM((PEER_HEADS, PEER_N_KEYS, t), dtype)
    return pl.pallas_call(
        _peer_kernel,
        grid=(s // t, n_exp // et),
        in_specs=[
            pl.BlockSpec((D_MODEL, t), lambda i, e: (0, i)),
            pl.BlockSpec((t, D_MODEL), lambda i, e: (i, 0)),
            pl.BlockSpec((PEER_HEADS * PEER_D_KEY, D_MODEL), lambda i, e: (0, 0)),
            pl.BlockSpec((PEER_HEADS, 2, PEER_N_KEYS, PEER_HALF), lambda i, e: (0, 0, 0, 0)),
            pl.BlockSpec((et, D_MODEL), lambda i, e: (e, 0)),
            pl.BlockSpec((D_MODEL, et), lambda i, e: (0, e)),
        ],
        out_specs=pl.BlockSpec((t, D_MODEL), lambda i, e: (i, 0)),
        out_shape=jax.ShapeDtypeStruct((s, D_MODEL), F32),
        scratch_shapes=[rt(F32), rt(F32), rt(BF16), rt(BF16),
                        pltpu.VMEM((et, t), BF16),
                        pltpu.VMEM((D_MODEL, t), F32)],
        compiler_params=pltpu.CompilerParams(
            dimension_semantics=("arbitrary", "arbitrary"), vmem_limit_bytes=VMEM_LIMIT_BYTES),
        name="peer",
    )(h2t, x2, wq_t, keys, u_bf, vt_bf)


def _block_diag(w):
    n, bi, bj = w.shape
    eye = jnp.eye(n, dtype=w.dtype)
    return (eye[:, None, :, None] * w[:, :, None, :]).reshape(n * bi, n * bj)


def kernel(x, norm_mix, w_in, q_norm, k_norm, conv_w, conv_b, rg_w_a, rg_b_a, rg_w_x, rg_b_x,
           rg_lambda, out_norm_sb, out_norm_rg, w_out, norm_ffn, peer_w_query, peer_sub_keys,
           peer_u, peer_v):
    bsz, s, d = x.shape
    assert bsz == 1 and d == D_MODEL
    assert s % max(INPROJ_TOKENS, ATT_Q, MIX_TOKENS, PEER_TOKENS) == 0
    depth = w_in.shape[0]
    x2d = x.reshape(s, d)

    head_id = jnp.arange(D_SB) // SB_HEAD_DIM
    bd = (head_id[:, None] == head_id[None, :]).astype(BF16)
    kidx = jnp.arange(ATT_K)
    tri = (kidx[None, :] > kidx[:, None]).astype(BF16)
    row = lambda v: v.reshape(1, -1)

    for l in range(depth):
        w = w_in[l]
        wtq = w[:, 0:D_SB].T.astype(BF16)
        wtv = w[:, 2 * D_SB:3 * D_SB].T.astype(BF16)
        wnat = jnp.concatenate([w[:, D_SB:2 * D_SB], w[:, 3 * D_SB:]], axis=1).astype(BF16)
        qg = jnp.tile(q_norm[l], SB_HEADS).reshape(D_SB, 1)
        kg = jnp.tile(k_norm[l], SB_HEADS).reshape(1, D_SB)
        qt, k, vt, xrg, grg = _inproj(x2d, row(norm_mix[l]), wnat, wtq, wtv, qg, kg, bd)

        ot = _attention(qt, k, vt, tri)

        x2, h2t = _mix(xrg, grg, ot, x2d, conv_w[l], row(conv_b[l]),
                      _block_diag(rg_w_a[l]).astype(BF16), row(rg_b_a[l]),
                      _block_diag(rg_w_x[l]).astype(BF16), row(rg_b_x[l]),
                      row(rg_lambda[l]), row(out_norm_sb[l]), row(out_norm_rg[l]),
                      w_out[l].astype(BF16), row(norm_ffn[l]))

        wq_t = peer_w_query[l].reshape(d, PEER_HEADS * PEER_D_KEY).T.astype(BF16)
        x2d = _peer(h2t, x2, wq_t, peer_sub_keys[l].astype(BF16),
                    peer_u[l].astype(BF16), peer_v[l].T.astype(BF16))
    return x2d.reshape(bsz, s, d)
```

```python
import functools

import jax
import jax.numpy as jnp
from jax import lax
from jax.experimental import pallas as pl
from jax.experimental.pallas import tpu as pltpu

F32 = jnp.float32
BF16 = jnp.bfloat16

D_MODEL = 1024
SB_HEADS = 8
SB_HEAD_DIM = 64
D_SB = SB_HEADS * SB_HEAD_DIM
D_RG = D_MODEL - D_SB
RG_BLOCKS = 8
RG_BLOCK_DIM = D_RG // RG_BLOCKS
CONV_WIDTH = 4
RG_C = 8.0
PEER_HEADS = 8
PEER_N_KEYS = 128
PEER_D_KEY = 256
PEER_HALF = PEER_D_KEY // 2
PEER_TOPK = 16
EPS = 1e-6

VMEM_LIMIT_BYTES = 56 * 1024 * 1024
SUBLANES = 8
LANES = 128

INPROJ_TOKENS = 512
ATT_Q = 512
ATT_K = 256
MIX_TOKENS = 256
PEER_TOKENS = 512
PEER_EXPERTS = 1024
PEER_CHUNK = 256

NEG_INF = float("-inf")
ATT_ZERO_WEIGHT = 110.0


def _nt_dot(a, b):
    return lax.dot_general(a, b, (((1,), (1,)), ((), ())), preferred_element_type=F32)


def _softplus(z):
    return jnp.maximum(z, 0.0) + jnp.log(1.0 + jnp.exp(-jnp.abs(z)))


def _gelu_tanh(x):
    c0 = 0.7978845608028654
    c1 = 0.044715 * c0
    half = 0.5 * x
    return half + half * jnp.tanh(x * (c1 * (x * x) + c0))


def _inproj_kernel(x_ref, gmix_ref, wnat_ref, wtq_ref, wtv_ref, qg_ref, kg_ref, bd_ref,
                   qt_ref, k_ref, vt_ref, xrg_ref, grg_ref):
    x = x_ref[...]
    ms = jnp.mean(x * x, axis=-1, keepdims=True)
    h = (x * lax.rsqrt(ms + EPS) * gmix_ref[...]).astype(BF16)

    nat = jnp.dot(h, wnat_ref[...], preferred_element_type=F32)
    k = nat[:, :D_SB]
    kk = k * k
    kk_hi = kk.astype(BF16)
    kk_lo = (kk - kk_hi.astype(F32)).astype(BF16)
    kss = (jnp.dot(kk_hi, bd_ref[...], preferred_element_type=F32)
           + jnp.dot(kk_lo, bd_ref[...], preferred_element_type=F32))
    k_ref[...] = (k * lax.rsqrt(kss * (1.0 / SB_HEAD_DIM) + EPS) * kg_ref[...]).astype(BF16)
    xrg_ref[...] = nat[:, D_SB:D_SB + D_RG]
    grg_ref[...] = nat[:, D_SB + D_RG:]

    t = x.shape[0]
    qt = _nt_dot(wtq_ref[...], h).reshape(SB_HEADS, SB_HEAD_DIM, t)
    qss = jnp.sum(qt * qt, axis=1, keepdims=True)
    qn = qt * lax.rsqrt(qss * (1.0 / SB_HEAD_DIM) + EPS)
    qn = qn.reshape(D_SB, t) * (qg_ref[...] * (SB_HEAD_DIM ** -0.5))
    qt_ref[...] = qn.astype(BF16)

    vt = _nt_dot(wtv_ref[...], h).astype(BF16)
    for c in range(t // ATT_K):
        vt_ref[c] = vt[:, c * ATT_K:(c + 1) * ATT_K]


def _inproj(x2d, gmix, wnat, wtq, wtv, qg, kg, bd):
    s = x2d.shape[0]
    t = INPROJ_TOKENS
    n_nat = wnat.shape[1]
    const = lambda shape: pl.BlockSpec(shape, lambda i: (0,) * len(shape))
    return pl.pallas_call(
        _inproj_kernel,
        grid=(s // t,),
        in_specs=[
            pl.BlockSpec((t, D_MODEL), lambda i: (i, 0)),
            const((1, D_MODEL)),
            const((D_MODEL, n_nat)),
            const((D_SB, D_MODEL)),
            const((D_SB, D_MODEL)),
            const((D_SB, 1)),
            const((1, D_SB)),
            const((D_SB, D_SB)),
        ],
        out_specs=[
            pl.BlockSpec((D_SB, t), lambda i: (0, i)),
            pl.BlockSpec((t, D_SB), lambda i: (i, 0)),
            pl.BlockSpec((t // ATT_K, D_SB, ATT_K), lambda i: (i, 0, 0)),
            pl.BlockSpec((t, D_RG), lambda i: (i, 0)),
            pl.BlockSpec((t, D_RG), lambda i: (i, 0)),
        ],
        out_shape=[
            jax.ShapeDtypeStruct((D_SB, s), BF16),
            jax.ShapeDtypeStruct((s, D_SB), BF16),
            jax.ShapeDtypeStruct((s // ATT_K, D_SB, ATT_K), BF16),
            jax.ShapeDtypeStruct((s, D_RG), F32),
            jax.ShapeDtypeStruct((s, D_RG), F32),
        ],
        compiler_params=pltpu.CompilerParams(
            dimension_semantics=("arbitrary",), vmem_limit_bytes=VMEM_LIMIT_BYTES),
        name="inproj",
    )(x2d, gmix, wnat, wtq, wtv, qg, kg, bd)


def _attn_kernel(qt_ref, k_ref, vt_ref, tri_ref, ot_ref, d_ref, s_ref, w_ref):
    hd = pl.program_id(0)
    j = pl.program_id(1)
    q = qt_ref[...]
    zero = jnp.zeros_like(q)
    qpad = jnp.where(hd % 2 == 0,
                     jnp.concatenate([q, zero], axis=0),
                     jnp.concatenate([zero, q], axis=0))
    tri = tri_ref[...]
    subs = ATT_Q // ATT_K
    q0 = j * ATT_Q

    def a_mm(sb):
        out = []
        for c in range(subs):
            ks = pl.multiple_of((sb * subs + c) * ATT_K, ATT_K)
            out.append(jnp.dot(k_ref[pl.ds(ks, ATT_K), :], qpad, preferred_element_type=F32))
        return out

    def a_ew(zs, sb, slot, masked):
        for c, z in enumerate(zs):
            sp = jnp.maximum(z, 0.0) + jnp.log(1.0 + jnp.exp(-jnp.abs(z)))
            d = z - sp
            if masked:
                kpos = (sb * subs + c) * ATT_K + lax.broadcasted_iota(jnp.int32, z.shape, 0)
                qpos = q0 + lax.broadcasted_iota(jnp.int32, z.shape, 1)
                valid = kpos < qpos
                sp = jnp.where(valid, sp, 0.0)
                d = jnp.where(valid, d, NEG_INF)
            d_ref[slot, c] = d
            s_ref[slot, c] = sp.astype(BF16)

    def b_mm(slot):
        return [jnp.dot(tri, s_ref[slot, c], preferred_element_type=F32) for c in range(subs)]

    def b_ew(sufs, slot, carry):
        for c in range(subs - 1, -1, -1):
            w_ref[slot, c] = jnp.exp(d_ref[slot, c] - sufs[c] - carry).astype(BF16)
            carry = carry + (sufs[c][0:1, :] + s_ref[slot, c, 0:1, :].astype(F32))
        return carry

    def c_mm(sb, slot, acc):
        for c in range(subs):
            acc = acc + jnp.dot(vt_ref[sb * subs + c], w_ref[slot, c], preferred_element_type=F32)
        return acc

    def tick(n, slot, carry, acc, do_a=True, do_b=True, do_c=True):
        if do_a:
            zs = a_mm(j - n)
        if do_b:
            sufs = b_mm(1 - slot)
        if do_c:
            acc = c_mm(j - n + 2, slot, acc)
        if do_a:
            a_ew(zs, j - n, slot, False)
        if do_b:
            carry = b_ew(sufs, 1 - slot, carry)
        return carry, acc

    n_steps = j + 1
    carry = jnp.zeros((1, ATT_Q), F32)
    acc = jnp.zeros((SB_HEAD_DIM, ATT_Q), F32)
    a_ew(a_mm(j), j, 0, True)

    def single(ca):
        carry, acc = tick(1, 1, *ca, do_a=False, do_c=False)
        return tick(2, 0, carry, acc, do_a=False, do_b=False)

    def multi(ca):
        carry, acc = tick(1, 1, *ca, do_c=False)

        n_pairs = (n_steps - 2) // 2

        def cond(state):
            p, done, _, _ = state
            return jnp.logical_and(p < n_pairs, done == 0)

        def body(state):
            p, _, carry, acc = state
            carry, acc = tick(2 * p + 2, 0, carry, acc)
            done = (jnp.min(carry) >= ATT_ZERO_WEIGHT).astype(jnp.int32)
            carry, acc = tick(2 * p + 3, 1, carry, acc)
            return p + 1, done, carry, acc

        _, done, carry, acc = lax.while_loop(cond, body, (jnp.int32(0), jnp.int32(0), carry, acc))

        def tail_even(ca):
            carry, acc = tick(n_steps, 0, *ca, do_a=False)
            return tick(n_steps + 1, 1, carry, acc, do_a=False, do_b=False)

        def tail_odd(ca):
            carry, acc = tick(n_steps - 1, 0, *ca)
            carry, acc = tick(n_steps, 1, carry, acc, do_a=False)
            return tick(n_steps + 1, 0, carry, acc, do_a=False, do_b=False)

        def tails(ca):
            return lax.cond(n_steps % 2 == 0, tail_even, tail_odd, ca)

        return lax.cond(done == 1, lambda ca: ca, tails, (carry, acc))

    carry, acc = lax.cond(n_steps == 1, single, multi, (carry, acc))
    ot_ref[...] = acc


def _attention(qt, k, vt, tri):
    s = k.shape[0]
    return pl.pallas_call(
        _attn_kernel,
        grid=(SB_HEADS, s // ATT_Q),
        in_specs=[
            pl.BlockSpec((SB_HEAD_DIM, ATT_Q), lambda h, j: (h, j)),
            pl.BlockSpec((s, 2 * SB_HEAD_DIM), lambda h, j: (0, h // 2)),
            pl.BlockSpec((s // ATT_K, SB_HEAD_DIM, ATT_K), lambda h, j: (0, h, 0)),
            pl.BlockSpec((ATT_K, ATT_K), lambda h, j: (0, 0)),
        ],
        out_specs=pl.BlockSpec((SB_HEAD_DIM, ATT_Q), lambda h, j: (h, j)),
        out_shape=jax.ShapeDtypeStruct((D_SB, s), F32),
        scratch_shapes=[
            pltpu.VMEM((2, ATT_Q // ATT_K, ATT_K, ATT_Q), F32),
            pltpu.VMEM((2, ATT_Q // ATT_K, ATT_K, ATT_Q), BF16),
            pltpu.VMEM((2, ATT_Q // ATT_K, ATT_K, ATT_Q), BF16),
        ],
        compiler_params=pltpu.CompilerParams(
            dimension_semantics=("arbitrary", "arbitrary"), vmem_limit_bytes=VMEM_LIMIT_BYTES),
        name="sb_attention",
    )(qt, k, vt, tri)


def _mix_kernel(xrg_ref, grg_ref, ot_ref, x_ref, cw_ref, cb_ref, wa_ref, ba_ref, wx_ref, bx_ref,
                lam_ref, nsb_ref, nrg_ref, wout_ref, nffn_ref,
                x2_ref, h2t_ref, xs_ref, hs_ref, hstate_ref):
    t = xrg_ref.shape[0]
    hist = SUBLANES

    @pl.when(pl.program_id(0) == 0)
    def _():
        xs_ref[0:hist, :] = jnp.zeros((hist, D_RG), F32)
        hstate_ref[...] = jnp.zeros_like(hstate_ref)

    xs_ref[hist:hist + t, :] = xrg_ref[...]
    y = cb_ref[...] + cw_ref[0:1, :] * xs_ref[hist - 3:hist - 3 + t, :]
    for jj in range(1, CONV_WIDTH):
        y = y + cw_ref[jj:jj + 1, :] * xs_ref[hist - 3 + jj:hist - 3 + jj + t, :]
    xs_ref[0:hist, :] = xs_ref[t:t + hist, :]

    yb = y.astype(BF16)
    r = jax.nn.sigmoid(jnp.dot(yb, wa_ref[...], preferred_element_type=F32) + ba_ref[...])
    ig = jax.nn.sigmoid(jnp.dot(yb, wx_ref[...], preferred_element_type=F32) + bx_ref[...])
    log_a = (-RG_C) * r * _softplus(-lam_ref[...])
    a = jnp.exp(log_a)
    b = jnp.sqrt(jnp.tanh(-log_a) * (1.0 + a * a)) * (ig * y)

    rowmod = lax.broadcasted_iota(jnp.int32, (t, D_RG), 0) % SUBLANES
    d = 1
    while d < SUBLANES:
        keep = rowmod >= d
        a_sh = pltpu.roll(a, d, axis=0)
        b_sh = pltpu.roll(b, d, axis=0)
        b = jnp.where(keep, b + a * b_sh, b)
        a = jnp.where(keep, a * a_sh, a)
        d *= 2
    hprev = hstate_ref[...]
    for g in range(t // SUBLANES):
        sl = slice(g * SUBLANES, (g + 1) * SUBLANES)
        hg = a[sl] * hprev + b[sl]
        hs_ref[sl, :] = hg
        hprev = jnp.broadcast_to(hg[SUBLANES - 1:SUBLANES, :], (SUBLANES, D_RG))
    hstate_ref[...] = hprev

    o_rg = hs_ref[...] * _gelu_tanh(grg_ref[...])
    o_sb = ot_ref[...].T

    def rms(v, g):
        return v * lax.rsqrt(jnp.mean(v * v, axis=-1, keepdims=True) + EPS) * g

    n_sb = rms(o_sb, nsb_ref[...]).astype(BF16)
    n_rg = rms(o_rg, nrg_ref[...]).astype(BF16)
    mix = (jnp.dot(n_sb, wout_ref[0:D_SB, :], preferred_element_type=F32)
           + jnp.dot(n_rg, wout_ref[D_SB:, :], preferred_element_type=F32))
    x2 = x_ref[...] + mix
    x2_ref[...] = x2
    h2t_ref[...] = rms(x2, nffn_ref[...]).T.astype(BF16)


def _mix(xrg, grg, ot, x2d, cw, cb, wa, ba, wx, bx, lam, nsb, nrg, wout, nffn):
    s = x2d.shape[0]
    t = MIX_TOKENS
    const = lambda shape: pl.BlockSpec(shape, lambda i: (0,) * len(shape))
    return pl.pallas_call(
        _mix_kernel,
        grid=(s // t,),
        in_specs=[
            pl.BlockSpec((t, D_RG), lambda i: (i, 0)),
            pl.BlockSpec((t, D_RG), lambda i: (i, 0)),
            pl.BlockSpec((D_SB, t), lambda i: (0, i)),
            pl.BlockSpec((t, D_MODEL), lambda i: (i, 0)),
            const((CONV_WIDTH, D_RG)), const((1, D_RG)),
            const((D_RG, D_RG)), const((1, D_RG)),
            const((D_RG, D_RG)), const((1, D_RG)),
            const((1, D_RG)), const((1, D_SB)), const((1, D_RG)),
            const((D_MODEL, D_MODEL)), const((1, D_MODEL)),
        ],
        out_specs=[
            pl.BlockSpec((t, D_MODEL), lambda i: (i, 0)),
            pl.BlockSpec((D_MODEL, t), lambda i: (0, i)),
        ],
        out_shape=[
            jax.ShapeDtypeStruct((s, D_MODEL), F32),
            jax.ShapeDtypeStruct((D_MODEL, s), BF16),
        ],
        scratch_shapes=[
            pltpu.VMEM((t + SUBLANES, D_RG), F32),
            pltpu.VMEM((t, D_RG), F32),
            pltpu.VMEM((SUBLANES, D_RG), F32),
        ],
        compiler_params=pltpu.CompilerParams(
            dimension_semantics=("arbitrary",), vmem_limit_bytes=VMEM_LIMIT_BYTES),
        name="rglru_outproj",
    )(xrg, grg, ot, x2d, cw, cb, wa, ba, wx, bx, lam, nsb, nrg, wout, nffn)


_N_RANK = PEER_TOPK + 1
_CAND_PAIRS = [(a, b) for a in range(_N_RANK) for b in range(_N_RANK)
               if (a + 1) * (b + 1) <= _N_RANK]


def _top_desc(s, n, with_rank=False):
    out = []
    rank = jnp.full(s.shape, float(s.shape[0]), F32) if with_rank else None
    for r in range(n):
        m = jnp.max(s, axis=0, keepdims=True)
        out.append(m)
        hit = s == m
        if with_rank and r < n - 1:
            rank = jnp.where(hit, float(r), rank)
        s = jnp.where(hit, NEG_INF, s)
    return (out, rank) if with_rank else out


def _peer_route(ht, wq_ref, keys_ref, e1_ref, n1_ref, e2_ref, r2_ref):
    for hd in range(PEER_HEADS):
        qt = jnp.dot(wq_ref[hd * PEER_D_KEY:(hd + 1) * PEER_D_KEY, :], ht,
                     preferred_element_type=F32).astype(BF16)
        s1 = jnp.dot(keys_ref[hd, 0], qt[:PEER_HALF], preferred_element_type=F32)
        s2 = jnp.dot(keys_ref[hd, 1], qt[PEER_HALF:], preferred_element_type=F32)
        top1 = _top_desc(s1, _N_RANK)
        top2, rank2 = _top_desc(s2, _N_RANK, with_rank=True)
        cand = jnp.concatenate([top1[a] + top2[b] for a, b in _CAND_PAIRS], axis=0)
        best = _top_desc(cand, _N_RANK)
        zsum = jnp.ones_like(best[0])
        for kk in range(1, PEER_TOPK):
            zsum = zsum + jnp.exp(best[kk] - best[0])
        tau = 0.5 * (best[PEER_TOPK - 1] + best[PEER_TOPK])
        theta = tau - s1
        count1 = jnp.zeros_like(s1)
        for r in range(PEER_TOPK):
            count1 = count1 + jnp.where(top2[r] > theta, 1.0, 0.0)
        e1_ref[hd] = jnp.exp(s1 - top1[0]) / zsum
        n1_ref[hd] = count1
        e2_ref[hd] = jnp.exp(s2 - top2[0]).astype(BF16)
        r2_ref[hd] = rank2.astype(BF16)


def _peer_kernel(ht_ref, x2_ref, wq_ref, keys_ref, u_ref, vt_ref, out_ref,
                 e1_ref, n1_ref, e2_ref, r2_ref, p_ref, acc_ref):
    e = pl.program_id(1)
    t = ht_ref.shape[1]
    n1 = PEER_EXPERTS // PEER_N_KEYS
    assert n1 == SUBLANES

    @pl.when(e == 0)
    def _():
        _peer_route(ht_ref[...], wq_ref, keys_ref, e1_ref, n1_ref, e2_ref, r2_ref)
        acc_ref[...] = jnp.zeros_like(acc_ref)

    i1_base = pl.multiple_of(e * n1, SUBLANES)
    ht = ht_ref[...]
    il_per_chunk = PEER_CHUNK // PEER_N_KEYS
    rows = 2 * SUBLANES
    n_row_tiles = PEER_N_KEYS // rows
    n_chunks = PEER_EXPERTS // PEER_CHUNK
    chunk = lambda ch: slice(ch * PEER_CHUNK, (ch + 1) * PEER_CHUNK)
    pre_all = jnp.dot(u_ref[...], ht, preferred_element_type=F32)
    pre = [pre_all[chunk(ch)] for ch in range(n_chunks)]
    for ch in range(n_chunks):
        for tc in range(t // LANES):
            ls = slice(tc * LANES, (tc + 1) * LANES)
            w = [[jnp.zeros((rows, LANES), BF16) for _ in range(n_row_tiles)]
                 for _ in range(il_per_chunk)]
            for hd in range(PEER_HEADS):
                cnt8 = n1_ref[hd, pl.ds(i1_base, n1), ls]
                e18 = e1_ref[hd, pl.ds(i1_base, n1), ls]
                cnt, e1 = [], []
                for ii in range(il_per_chunk):
                    il = ch * il_per_chunk + ii
                    cnt.append(jnp.broadcast_to(cnt8[il:il + 1], (rows, LANES)).astype(BF16))
                    e1.append(jnp.broadcast_to(e18[il:il + 1], (rows, LANES)).astype(BF16))
                for rt in range(n_row_tiles):
                    rs = slice(rt * rows, (rt + 1) * rows)
                    r2 = r2_ref[hd, rs, ls]
                    e2 = e2_ref[hd, rs, ls]
                    for ii in range(il_per_chunk):
                        w[ii][rt] = w[ii][rt] + jnp.where(r2 < cnt[ii], e1[ii] * e2, 0.0)
            act = _gelu_tanh(pre[ch][:, ls]).astype(BF16)
            for ii in range(il_per_chunk):
                for rt in range(n_row_tiles):
                    r0 = ii * PEER_N_KEYS + rt * rows
                    p_ref[ch * PEER_CHUNK + r0:ch * PEER_CHUNK + r0 + rows, ls] = (
                        w[ii][rt] * act[r0:r0 + rows])
    acc_ref[...] += jnp.dot(vt_ref[...], p_ref[...], preferred_element_type=F32)

    @pl.when(e == pl.num_programs(1) - 1)
    def _():
        out_ref[...] = x2_ref[...] + acc_ref[...].T


def _peer(h2t, x2, wq_t, keys, u_bf, vt_bf):
    s = h2t.shape[1]
    t = PEER_TOKENS
    n_exp = u_bf.shape[0]
    et = PEER_EXPERTS
    rt = lambda dtype: pltpu.VMEM((PEER_HEADS, PEER_N_KEYS, t), dtype)
    return pl.pallas_call(
        _peer_kernel,
        grid=(s // t, n_exp // et),
        in_specs=[
            pl.BlockSpec((D_MODEL, t), lambda i, e: (0, i)),
            pl.BlockSpec((t, D_MODEL), lambda i, e: (i, 0)),
            pl.BlockSpec((PEER_HEADS * PEER_D_KEY, D_MODEL), lambda i, e: (0, 0)),
            pl.BlockSpec((PEER_HEADS, 2, PEER_N_KEYS, PEER_HALF), lambda i, e: (0, 0, 0, 0)),
            pl.BlockSpec((et, D_MODEL), lambda i, e: (e, 0)),
            pl.BlockSpec((D_MODEL, et), lambda i, e: (0, e)),
        ],
        out_specs=pl.BlockSpec((t, D_MODEL), lambda i, e: (i, 0)),
        out_shape=jax.ShapeDtypeStruct((s, D_MODEL), F32),
        scratch_shapes=[rt(F32), rt(F32), rt(BF16), rt(BF16),
                        pltpu.VMEM((et, t), BF16),
                        pltpu.VMEM((D_MODEL, t), F32)],
        compiler_params=pltpu.CompilerParams(
            dimension_semantics=("arbitrary", "arbitrary"), vmem_limit_bytes=VMEM_LIMIT_BYTES),
        name="peer",
    )(h2t, x2, wq_t, keys, u_bf, vt_bf)


def _block_diag(w):
    n, bi, bj = w.shape
    eye = jnp.eye(n, dtype=w.dtype)
    return (eye[:, None, :, None] * w[:, :, None, :]).reshape(n * bi, n * bj)


def kernel(x, norm_mix, w_in, q_norm, k_norm, conv_w, conv_b, rg_w_a, rg_b_a, rg_w_x, rg_b_x,
           rg_lambda, out_norm_sb, out_norm_rg, w_out, norm_ffn, peer_w_query, peer_sub_keys,
           peer_u, peer_v):
    bsz, s, d = x.shape
    assert bsz == 1 and d == D_MODEL
    assert s % max(INPROJ_TOKENS, ATT_Q, MIX_TOKENS, PEER_TOKENS) == 0
    depth = w_in.shape[0]
    x2d = x.reshape(s, d)

    head_id = jnp.arange(D_SB) // SB_HEAD_DIM
    bd = (head_id[:, None] == head_id[None, :]).astype(BF16)
    kidx = jnp.arange(ATT_K)
    tri = (kidx[None, :] > kidx[:, None]).astype(BF16)
    row = lambda v: v.reshape(1, -1)

    for l in range(depth):
        w = w_in[l]
        wtq = w[:, 0:D_SB].T.astype(BF16)
        wtv = w[:, 2 * D_SB:3 * D_SB].T.astype(BF16)
        wnat = jnp.concatenate([w[:, D_SB:2 * D_SB], w[:, 3 * D_SB:]], axis=1).astype(BF16)
        qg = jnp.tile(q_norm[l], SB_HEADS).reshape(D_SB, 1)
        kg = jnp.tile(k_norm[l], SB_HEADS).reshape(1, D_SB)
        qt, k, vt, xrg, grg = _inproj(x2d, row(norm_mix[l]), wnat, wtq, wtv, qg, kg, bd)

        ot = _attention(qt, k, vt, tri)

        x2, h2t = _mix(xrg, grg, ot, x2d, conv_w[l], row(conv_b[l]),
                      _block_diag(rg_w_a[l]).astype(BF16), row(rg_b_a[l]),
                      _block_diag(rg_w_x[l]).astype(BF16), row(rg_b_x[l]),
                      row(rg_lambda[l]), row(out_norm_sb[l]), row(out_norm_rg[l]),
                      w_out[l].astype(BF16), row(norm_ffn[l]))

        wq_t = peer_w_query[l].reshape(d, PEER_HEADS * PEER_D_KEY).T.astype(BF16)
        x2d = _peer(h2t, x2, wq_t, peer_sub_keys[l].astype(BF16),
                    peer_u[l].astype(BF16), peer_v[l].T.astype(BF16))
    return x2d.reshape(bsz, s, d)
```

```python
import jax
import jax.numpy as jnp
from jax import lax
from jax.experimental import pallas as pl
from jax.experimental.pallas import tpu as pltpu

F32 = jnp.float32
BF16 = jnp.bfloat16

D_MODEL = 1024
SB_HEADS = 8
SB_HEAD_DIM = 64
D_SB = SB_HEADS * SB_HEAD_DIM
D_RG = D_MODEL - D_SB
RG_BLOCKS = 8
RG_BLOCK_DIM = D_RG // RG_BLOCKS
CONV_WIDTH = 4
RG_C = 8.0
PEER_HEADS = 8
PEER_N_KEYS = 128
PEER_D_KEY = 256
PEER_HALF = PEER_D_KEY // 2
PEER_TOPK = 16
EPS = 1e-6

VMEM_LIMIT_BYTES = 56 * 1024 * 1024
SUBLANES = 8
LANES = 128

INPROJ_TOKENS = 512
ATT_Q = 512
ATT_K = 256
MIX_TOKENS = 256
PEER_TOKENS = 512
PEER_EXPERTS = 1024
PEER_CHUNK = 256

NEG_INF = float("-inf")
ATT_ZERO_WEIGHT = 110.0


def _nt_dot(a, b):
    return lax.dot_general(a, b, (((1,), (1,)), ((), ())), preferred_element_type=F32)


def _softplus(z):
    return jnp.maximum(z, 0.0) + jnp.log(1.0 + jnp.exp(-jnp.abs(z)))


def _gelu_tanh(x):
    c0 = 0.7978845608028654
    c1 = 0.044715 * c0
    half = 0.5 * x
    return half + half * jnp.tanh(x * (c1 * (x * x) + c0))


def _inproj_kernel(x_ref, gmix_ref, wnat_ref, wtq_ref, wtv_ref, qg_ref, kg_ref, bd_ref,
                   qt_ref, k_ref, vt_ref, xrg_ref, grg_ref):
    x = x_ref[...]
    ms = jnp.mean(x * x, axis=-1, keepdims=True)
    h = (x * lax.rsqrt(ms + EPS) * gmix_ref[...]).astype(BF16)

    nat = jnp.dot(h, wnat_ref[...], preferred_element_type=F32)
    k = nat[:, :D_SB]
    kk = k * k
    kk_hi = kk.astype(BF16)
    kk_lo = (kk - kk_hi.astype(F32)).astype(BF16)
    kss = (jnp.dot(kk_hi, bd_ref[...], preferred_element_type=F32)
           + jnp.dot(kk_lo, bd_ref[...], preferred_element_type=F32))
    k_ref[...] = (k * lax.rsqrt(kss * (1.0 / SB_HEAD_DIM) + EPS) * kg_ref[...]).astype(BF16)
    xrg_ref[...] = nat[:, D_SB:D_SB + D_RG]
    grg_ref[...] = nat[:, D_SB + D_RG:]

    t = x.shape[0]
    qt = _nt_dot(wtq_ref[...], h).reshape(SB_HEADS, SB_HEAD_DIM, t)
    qss = jnp.sum(qt * qt, axis=1, keepdims=True)
    qn = qt * lax.rsqrt(qss * (1.0 / SB_HEAD_DIM) + EPS)
    qn = qn.reshape(D_SB, t) * (qg_ref[...] * (SB_HEAD_DIM ** -0.5))
    qt_ref[...] = qn.astype(BF16)

    vt = _nt_dot(wtv_ref[...], h).astype(BF16)
    for c in range(t // ATT_K):
        vt_ref[c] = vt[:, c * ATT_K:(c + 1) * ATT_K]


def _inproj(x2d, gmix, wnat, wtq, wtv, qg, kg, bd):
    s = x2d.shape[0]
    t = INPROJ_TOKENS
    n_nat = wnat.shape[1]
    const = lambda shape: pl.BlockSpec(shape, lambda i: (0,) * len(shape))
    return pl.pallas_call(
        _inproj_kernel,
        grid=(s // t,),
        in_specs=[
            pl.BlockSpec((t, D_MODEL), lambda i: (i, 0)),
            const((1, D_MODEL)),
            const((D_MODEL, n_nat)),
            const((D_SB, D_MODEL)),
            const((D_SB, D_MODEL)),
            const((D_SB, 1)),
            const((1, D_SB)),
            const((D_SB, D_SB)),
        ],
        out_specs=[
            pl.BlockSpec((D_SB, t), lambda i: (0, i)),
            pl.BlockSpec((t, D_SB), lambda i: (i, 0)),
            pl.BlockSpec((t // ATT_K, D_SB, ATT_K), lambda i: (i, 0, 0)),
            pl.BlockSpec((t, D_RG), lambda i: (i, 0)),
            pl.BlockSpec((t, D_RG), lambda i: (i, 0)),
        ],
        out_shape=[
            jax.ShapeDtypeStruct((D_SB, s), BF16),
            jax.ShapeDtypeStruct((s, D_SB), BF16),
            jax.ShapeDtypeStruct((s // ATT_K, D_SB, ATT_K), BF16),
            jax.ShapeDtypeStruct((s, D_RG), F32),
            jax.ShapeDtypeStruct((s, D_RG), F32),
        ],
        compiler_params=pltpu.CompilerParams(
            dimension_semantics=("arbitrary",), vmem_limit_bytes=VMEM_LIMIT_BYTES),
        name="inproj",
    )(x2d, gmix, wnat, wtq, wtv, qg, kg, bd)


def _attn_kernel(qt_ref, k_ref, vt_ref, tri_ref, ot_ref, d_ref, s_ref, w_ref):
    hd = pl.program_id(0)
    j = pl.program_id(1)
    q = qt_ref[...]
    zero = jnp.zeros_like(q)
    qpad = jnp.where(hd % 2 == 0,
                     jnp.concatenate([q, zero], axis=0),
                     jnp.concatenate([zero, q], axis=0))
    tri = tri_ref[...]
    subs = ATT_Q // ATT_K
    q0 = j * ATT_Q

    def a_mm(sb):
        out = []
        for c in range(subs):
            ks = pl.multiple_of((sb * subs + c) * ATT_K, ATT_K)
            out.append(jnp.dot(k_ref[pl.ds(ks, ATT_K), :], qpad, preferred_element_type=F32))
        return out

    def a_ew(zs, sb, slot, masked):
        for c, z in enumerate(zs):
            sp = jnp.maximum(z, 0.0) + jnp.log(1.0 + jnp.exp(-jnp.abs(z)))
            d = z - sp
            if masked:
                kpos = (sb * subs + c) * ATT_K + lax.broadcasted_iota(jnp.int32, z.shape, 0)
                qpos = q0 + lax.broadcasted_iota(jnp.int32, z.shape, 1)
                valid = kpos < qpos
                sp = jnp.where(valid, sp, 0.0)
                d = jnp.where(valid, d, NEG_INF)
            d_ref[slot, c] = d
            s_ref[slot, c] = sp.astype(BF16)

    def b_mm(slot):
        return [jnp.dot(tri, s_ref[slot, c], preferred_element_type=F32) for c in range(subs)]

    def b_ew(sufs, slot, carry):
        for c in range(subs - 1, -1, -1):
            w_ref[slot, c] = jnp.exp(d_ref[slot, c] - sufs[c] - carry).astype(BF16)
            carry = carry + (sufs[c][0:1, :] + s_ref[slot, c, 0:1, :].astype(F32))
        return carry

    def c_mm(sb, slot, acc):
        for c in range(subs):
            acc = acc + jnp.dot(vt_ref[sb * subs + c], w_ref[slot, c], preferred_element_type=F32)
        return acc

    def tick(n, slot, carry, acc, do_a=True, do_b=True, do_c=True):
        if do_a:
            zs = a_mm(j - n)
        if do_b:
            sufs = b_mm(1 - slot)
        if do_c:
            acc = c_mm(j - n + 2, slot, acc)
        if do_a:
            a_ew(zs, j - n, slot, False)
        if do_b:
            carry = b_ew(sufs, 1 - slot, carry)
        return carry, acc

    n_steps = j + 1
    carry = jnp.zeros((1, ATT_Q), F32)
    acc = jnp.zeros((SB_HEAD_DIM, ATT_Q), F32)
    a_ew(a_mm(j), j, 0, True)

    def single(ca):
        carry, acc = tick(1, 1, *ca, do_a=False, do_c=False)
        return tick(2, 0, carry, acc, do_a=False, do_b=False)

    def multi(ca):
        carry, acc = tick(1, 1, *ca, do_c=False)

        n_pairs = (n_steps - 2) // 2

        def cond(state):
            p, done, _, _ = state
            return jnp.logical_and(p < n_pairs, done == 0)

        def body(state):
            p, _, carry, acc = state
            carry, acc = tick(2 * p + 2, 0, carry, acc)
            done = (jnp.min(carry) >= ATT_ZERO_WEIGHT).astype(jnp.int32)
            carry, acc = tick(2 * p + 3, 1, carry, acc)
            return p + 1, done, carry, acc

        _, done, carry, acc = lax.while_loop(cond, body, (jnp.int32(0), jnp.int32(0), carry, acc))

        def tail_even(ca):
            carry, acc = tick(n_steps, 0, *ca, do_a=False)
            return tick(n_steps + 1, 1, carry, acc, do_a=False, do_b=False)

        def tail_odd(ca):
            carry, acc = tick(n_steps - 1, 0, *ca)
            carry, acc = tick(n_steps, 1, carry, acc, do_a=False)
            return tick(n_steps + 1, 0, carry, acc, do_a=False, do_b=False)

        def tails(ca):
            return lax.cond(n_steps % 2 == 0, tail_even, tail_odd, ca)

        return lax.cond(done == 1, lambda ca: ca, tails, (carry, acc))

    carry, acc = lax.cond(n_steps == 1, single, multi, (carry, acc))
    ot_ref[...] = acc


def _attention(qt, k, vt, tri):
    s = k.shape[0]
    return pl.pallas_call(
        _attn_kernel,
        grid=(SB_HEADS, s // ATT_Q),
        in_specs=[
            pl.BlockSpec((SB_HEAD_DIM, ATT_Q), lambda h, j: (h, j)),
            pl.BlockSpec((s, 2 * SB_HEAD_DIM), lambda h, j: (0, h // 2)),
            pl.BlockSpec((s // ATT_K, SB_HEAD_DIM, ATT_K), lambda h, j: (0, h, 0)),
            pl.BlockSpec((ATT_K, ATT_K), lambda h, j: (0, 0)),
        ],
        out_specs=pl.BlockSpec((SB_HEAD_DIM, ATT_Q), lambda h, j: (h, j)),
        out_shape=jax.ShapeDtypeStruct((D_SB, s), F32),
        scratch_shapes=[
            pltpu.VMEM((2, ATT_Q // ATT_K, ATT_K, ATT_Q), F32),
            pltpu.VMEM((2, ATT_Q // ATT_K, ATT_K, ATT_Q), BF16),
            pltpu.VMEM((2, ATT_Q // ATT_K, ATT_K, ATT_Q), BF16),
        ],
        compiler_params=pltpu.CompilerParams(
            dimension_semantics=("arbitrary", "arbitrary"), vmem_limit_bytes=VMEM_LIMIT_BYTES),
        name="sb_attention",
    )(qt, k, vt, tri)


def _mix_kernel(xrg_ref, grg_ref, ot_ref, x_ref, cw_ref, cb_ref, wa_ref, ba_ref, wx_ref, bx_ref,
                lam_ref, nsb_ref, nrg_ref, wout_ref, nffn_ref,
                x2_ref, h2t_ref, xs_ref, hs_ref, hstate_ref):
    t = xrg_ref.shape[0]
    hist = SUBLANES

    @pl.when(pl.program_id(0) == 0)
    def _():
        xs_ref[0:hist, :] = jnp.zeros((hist, D_RG), F32)
        hstate_ref[...] = jnp.zeros_like(hstate_ref)

    xs_ref[hist:hist + t, :] = xrg_ref[...]
    y = cb_ref[...] + cw_ref[0:1, :] * xs_ref[hist - 3:hist - 3 + t, :]
    for jj in range(1, CONV_WIDTH):
        y = y + cw_ref[jj:jj + 1, :] * xs_ref[hist - 3 + jj:hist - 3 + jj + t, :]
    xs_ref[0:hist, :] = xs_ref[t:t + hist, :]

    yb = y.astype(BF16)
    r = jax.nn.sigmoid(jnp.dot(yb, wa_ref[...], preferred_element_type=F32) + ba_ref[...])
    ig = jax.nn.sigmoid(jnp.dot(yb, wx_ref[...], preferred_element_type=F32) + bx_ref[...])
    log_a = (-RG_C) * r * _softplus(-lam_ref[...])
    a = jnp.exp(log_a)
    b = jnp.sqrt(jnp.tanh(-log_a) * (1.0 + a * a)) * (ig * y)

    rowmod = lax.broadcasted_iota(jnp.int32, (t, D_RG), 0) % SUBLANES
    d = 1
    while d < SUBLANES:
        keep = rowmod >= d
        a_sh = pltpu.roll(a, d, axis=0)
        b_sh = pltpu.roll(b, d, axis=0)
        b = jnp.where(keep, b + a * b_sh, b)
        a = jnp.where(keep, a * a_sh, a)
        d *= 2
    hprev = hstate_ref[...]
    for g in range(t // SUBLANES):
        sl = slice(g * SUBLANES, (g + 1) * SUBLANES)
        hg = a[sl] * hprev + b[sl]
        hs_ref[sl, :] = hg
        hprev = jnp.broadcast_to(hg[SUBLANES - 1:SUBLANES, :], (SUBLANES, D_RG))
    hstate_ref[...] = hprev

    o_rg = hs_ref[...] * _gelu_tanh(grg_ref[...])
    o_sb = ot_ref[...].T

    def rms(v, g):
        return v * lax.rsqrt(jnp.mean(v * v, axis=-1, keepdims=True) + EPS) * g

    n_sb = rms(o_sb, nsb_ref[...]).astype(BF16)
    n_rg = rms(o_rg, nrg_ref[...]).astype(BF16)
    mix = (jnp.dot(n_sb, wout_ref[0:D_SB, :], preferred_element_type=F32)
           + jnp.dot(n_rg, wout_ref[D_SB:, :], preferred_element_type=F32))
    x2 = x_ref[...] + mix
    x2_ref[...] = x2
    h2t_ref[...] = rms(x2, nffn_ref[...]).T.astype(BF16)


def _mix(xrg, grg, ot, x2d, cw, cb, wa, ba, wx, bx, lam, nsb, nrg, wout, nffn):
    s = x2d.shape[0]
    t = MIX_TOKENS
    const = lambda shape: pl.BlockSpec(shape, lambda i: (0,) * len(shape))
    return pl.pallas_call(
        _mix_kernel,
        grid=(s // t,),
        in_specs=[
            pl.BlockSpec((t, D_RG), lambda i: (i, 0)),
            pl.BlockSpec((t, D_RG), lambda i: (i, 0)),
            pl.BlockSpec((D_SB, t), lambda i: (0, i)),
            pl.BlockSpec((t, D_MODEL), lambda i: (i, 0)),
            const((CONV_WIDTH, D_RG)), const((1, D_RG)),
            const((D_RG, D_RG)), const((1, D_RG)),
            const((D_RG, D_RG)), const((1, D_RG)),
            const((1, D_RG)), const((1, D_SB)), const((1, D_RG)),
            const((D_MODEL, D_MODEL)), const((1, D_MODEL)),
        ],
        out_specs=[
            pl.BlockSpec((t, D_MODEL), lambda i: (i, 0)),
            pl.BlockSpec((D_MODEL, t), lambda i: (0, i)),
        ],
        out_shape=[
            jax.ShapeDtypeStruct((s, D_MODEL), F32),
            jax.ShapeDtypeStruct((D_MODEL, s), BF16),
        ],
        scratch_shapes=[
            pltpu.VMEM((t + SUBLANES, D_RG), F32),
            pltpu.VMEM((t, D_RG), F32),
            pltpu.VMEM((SUBLANES, D_RG), F32),
        ],
        compiler_params=pltpu.CompilerParams(
            dimension_semantics=("arbitrary",), vmem_limit_bytes=VMEM_LIMIT_BYTES),
        name="rglru_outproj",
    )(xrg, grg, ot, x2d, cw, cb, wa, ba, wx, bx, lam, nsb, nrg, wout, nffn)


_N_RANK = PEER_TOPK + 1
_CAND_PAIRS = [(a, b) for a in range(_N_RANK) for b in range(_N_RANK)
               if (a + 1) * (b + 1) <= _N_RANK]


def _top_desc(s, n):
    out = []
    for r in range(n):
        m = jnp.max(s, axis=0, keepdims=True)
        out.append(m)
        if r + 1 < n:
            s = jnp.where(s == m, NEG_INF, s)
    return out


def _peer_route(ht, wq_ref, keys_ref, e1_ref, th_ref, e2_ref, s2_ref):
    for hd in range(PEER_HEADS):
        qt = jnp.dot(wq_ref[hd * PEER_D_KEY:(hd + 1) * PEER_D_KEY, :], ht,
                     preferred_element_type=F32).astype(BF16)
        s1 = jnp.dot(keys_ref[hd, 0], qt[:PEER_HALF], preferred_element_type=F32)
        s2 = jnp.dot(keys_ref[hd, 1], qt[PEER_HALF:], preferred_element_type=F32)
        top1 = _top_desc(s1, _N_RANK)
        top2 = _top_desc(s2, _N_RANK)
        cand = jnp.concatenate([top1[a] + top2[b] for a, b in _CAND_PAIRS], axis=0)
        best = _top_desc(cand, _N_RANK)
        zsum = jnp.ones_like(best[0])
        for kk in range(1, PEER_TOPK):
            zsum = zsum + jnp.exp(best[kk] - best[0])
        tau = 0.5 * (best[PEER_TOPK - 1] + best[PEER_TOPK])
        e1_ref[hd] = jnp.exp(s1 - top1[0]) / zsum
        th_ref[hd] = tau - s1
        e2_ref[hd] = jnp.exp(s2 - top2[0])
        s2_ref[hd] = s2


def _peer_kernel(ht_ref, x2_ref, wq_ref, keys_ref, u_ref, vt_ref, out_ref,
                 e1_ref, th_ref, e2_ref, s2_ref, p_ref, acc_ref):
    e = pl.program_id(1)
    t = ht_ref.shape[1]
    n1 = PEER_EXPERTS // PEER_N_KEYS
    assert n1 == SUBLANES

    @pl.when(e == 0)
    def _():
        _peer_route(ht_ref[...], wq_ref, keys_ref, e1_ref, th_ref, e2_ref, s2_ref)
        acc_ref[...] = jnp.zeros_like(acc_ref)

    i1_base = pl.multiple_of(e * n1, SUBLANES)
    il_per_chunk = PEER_CHUNK // PEER_N_KEYS
    half = PEER_N_KEYS // 2
    n_chunks = PEER_EXPERTS // PEER_CHUNK
    chunk = lambda ch: slice(ch * PEER_CHUNK, (ch + 1) * PEER_CHUNK)
    pre_all = jnp.dot(u_ref[...], ht_ref[...], preferred_element_type=F32)
    for ch in range(n_chunks):
        pre = pre_all[chunk(ch)]
        for tc in range(t // LANES):
            ls = slice(tc * LANES, (tc + 1) * LANES)
            th = [th_ref[hd, pl.ds(i1_base, n1), ls] for hd in range(PEER_HEADS)]
            e1 = [e1_ref[hd, pl.ds(i1_base, n1), ls] for hd in range(PEER_HEADS)]
            act = _gelu_tanh(pre[:, ls])
            for hf in range(2):
                ks = slice(hf * half, (hf + 1) * half)
                w = [jnp.zeros((half, LANES), F32) for _ in range(il_per_chunk)]
                for hd in range(PEER_HEADS):
                    s2 = s2_ref[hd, ks, ls]
                    e2 = e2_ref[hd, ks, ls]
                    for ii in range(il_per_chunk):
                        il = ch * il_per_chunk + ii
                        w[ii] = w[ii] + jnp.where(s2 > th[hd][il:il + 1], e1[hd][il:il + 1] * e2, 0.0)
                for ii in range(il_per_chunk):
                    r0 = ii * PEER_N_KEYS + hf * half
                    p_ref[ch * PEER_CHUNK + r0:ch * PEER_CHUNK + r0 + half, ls] = (
                        w[ii] * act[r0:r0 + half]).astype(BF16)
    acc_ref[...] += jnp.dot(vt_ref[...], p_ref[...], preferred_element_type=F32)

    @pl.when(e == pl.num_programs(1) - 1)
    def _():
        out_ref[...] = x2_ref[...] + acc_ref[...].T


def _peer(h2t, x2, wq_t, keys, u_bf, vt_bf):
    s = h2t.shape[1]
    t = PEER_TOKENS
    n_exp = u_bf.shape[0]
    et = PEER_EXPERTS
    rt = lambda: pltpu.VMEM((PEER_HEADS, PEER_N_KEYS, t), F32)
    return pl.pallas_call(
        _peer_kernel,
        grid=(s // t, n_exp // et),
        in_specs=[
            pl.BlockSpec((D_MODEL, t), lambda i, e: (0, i)),
            pl.BlockSpec((t, D_MODEL), lambda i, e: (i, 0)),
            pl.BlockSpec((PEER_HEADS * PEER_D_KEY, D_MODEL), lambda i, e: (0, 0)),
            pl.BlockSpec((PEER_HEADS, 2, PEER_N_KEYS, PEER_HALF), lambda i, e: (0, 0, 0, 0)),
            pl.BlockSpec((et, D_MODEL), lambda i, e: (e, 0)),
            pl.BlockSpec((D_MODEL, et), lambda i, e: (0, e)),
        ],
        out_specs=pl.BlockSpec((t, D_MODEL), lambda i, e: (i, 0)),
        out_shape=jax.ShapeDtypeStruct((s, D_MODEL), F32),
        scratch_shapes=[rt(), rt(), rt(), rt(),
                        pltpu.VMEM((et, t), BF16),
                        pltpu.VMEM((D_MODEL, t), F32)],
        compiler_params=pltpu.CompilerParams(
            dimension_semantics=("arbitrary", "arbitrary"), vmem_limit_bytes=VMEM_LIMIT_BYTES),
        name="peer",
    )(h2t, x2, wq_t, keys, u_bf, vt_bf)


def _block_diag(w):
    n, bi, bj = w.shape
    eye = jnp.eye(n, dtype=w.dtype)
    return (eye[:, None, :, None] * w[:, :, None, :]).reshape(n * bi, n * bj)


def kernel(x, norm_mix, w_in, q_norm, k_norm, conv_w, conv_b, rg_w_a, rg_b_a, rg_w_x, rg_b_x,
           rg_lambda, out_norm_sb, out_norm_rg, w_out, norm_ffn, peer_w_query, peer_sub_keys,
           peer_u, peer_v):
    bsz, s, d = x.shape
    assert bsz == 1 and d == D_MODEL
    assert s % max(INPROJ_TOKENS, ATT_Q, MIX_TOKENS, PEER_TOKENS) == 0
    depth = w_in.shape[0]
    x2d = x.reshape(s, d)

    head_id = jnp.arange(D_SB) // SB_HEAD_DIM
    bd = (head_id[:, None] == head_id[None, :]).astype(BF16)
    kidx = jnp.arange(ATT_K)
    tri = (kidx[None, :] > kidx[:, None]).astype(BF16)
    row = lambda v: v.reshape(1, -1)

    for l in range(depth):
        w = w_in[l]
        wtq = w[:, 0:D_SB].T.astype(BF16)
        wtv = w[:, 2 * D_SB:3 * D_SB].T.astype(BF16)
        wnat = jnp.concatenate([w[:, D_SB:2 * D_SB], w[:, 3 * D_SB:]], axis=1).astype(BF16)
        qg = jnp.tile(q_norm[l], SB_HEADS).reshape(D_SB, 1)
        kg = jnp.tile(k_norm[l], SB_HEADS).reshape(1, D_SB)
        qt, k, vt, xrg, grg = _inproj(x2d, row(norm_mix[l]), wnat, wtq, wtv, qg, kg, bd)

        ot = _attention(qt, k, vt, tri)

        x2, h2t = _mix(xrg, grg, ot, x2d, conv_w[l], row(conv_b[l]),
                       _block_diag(rg_w_a[l]).astype(BF16), row(rg_b_a[l]),
                       _block_diag(rg_w_x[l]).astype(BF16), row(rg_b_x[l]),
                       row(rg_lambda[l]), row(out_norm_sb[l]), row(out_norm_rg[l]),
                       w_out[l].astype(BF16), row(norm_ffn[l]))

        wq_t = peer_w_query[l].reshape(d, PEER_HEADS * PEER_D_KEY).T.astype(BF16)
        x2d = _peer(h2t, x2, wq_t, peer_sub_keys[l].astype(BF16),
                    peer_u[l].astype(BF16), peer_v[l].T.astype(BF16))
    return x2d.reshape(bsz, s, d)
```

```python
import jax
import jax.numpy as jnp
from jax import lax
from jax.experimental import pallas as pl
from jax.experimental.pallas import tpu as pltpu

F32 = jnp.float32
BF16 = jnp.bfloat16

D_MODEL = 1024
SB_HEADS = 8
SB_HEAD_DIM = 64
D_SB = SB_HEADS * SB_HEAD_DIM
D_RG = D_MODEL - D_SB
RG_BLOCKS = 8
RG_BLOCK_DIM = D_RG // RG_BLOCKS
CONV_WIDTH = 4
RG_C = 8.0
PEER_HEADS = 8
PEER_N_KEYS = 128
PEER_D_KEY = 256
PEER_HALF = PEER_D_KEY // 2
PEER_TOPK = 16
EPS = 1e-6

VMEM_LIMIT_BYTES = 56 * 1024 * 1024
SUBLANES = 8
LANES = 128

INPROJ_TOKENS = 1024
ATT_Q = 256
ATT_K = 256
MIX_TOKENS = 512
PEER_TOKENS = 512
PEER_EXPERTS = 1024
PEER_CHUNK = 256

NEG_INF = float("-inf")
ATT_ZERO_WEIGHT = 110.0


def _nt_dot(a, b):
    return lax.dot_general(a, b, (((1,), (1,)), ((), ())), preferred_element_type=F32)


def _softplus(z):
    return jnp.maximum(z, 0.0) + jnp.log(1.0 + jnp.exp(-jnp.abs(z)))


def _gelu_tanh(x):
    c0 = 0.7978845608028654
    c1 = 0.044715 * c0
    half = 0.5 * x
    return half + half * jnp.tanh(x * (c1 * (x * x) + c0))


def _inproj_kernel(x_ref, gmix_ref, wnat_ref, wtq_ref, wtv_ref, qg_ref, kg_ref, bd_ref,
                   qt_ref, k_ref, vt_ref, xrg_ref, grg_ref):
    x = x_ref[...]
    ms = jnp.mean(x * x, axis=-1, keepdims=True)
    h = (x * lax.rsqrt(ms + EPS) * gmix_ref[...]).astype(BF16)

    nat = jnp.dot(h, wnat_ref[...], preferred_element_type=F32)
    k = nat[:, :D_SB]
    kk = k * k
    kk_hi = kk.astype(BF16)
    kk_lo = (kk - kk_hi.astype(F32)).astype(BF16)
    kss = (jnp.dot(kk_hi, bd_ref[...], preferred_element_type=F32)
           + jnp.dot(kk_lo, bd_ref[...], preferred_element_type=F32))
    k_ref[...] = (k * lax.rsqrt(kss * (1.0 / SB_HEAD_DIM) + EPS) * kg_ref[...]).astype(BF16)
    xrg_ref[...] = nat[:, D_SB:D_SB + D_RG]
    grg_ref[...] = nat[:, D_SB + D_RG:]

    t = x.shape[0]
    qt = _nt_dot(wtq_ref[...], h).reshape(SB_HEADS, SB_HEAD_DIM, t)
    qss = jnp.sum(qt * qt, axis=1, keepdims=True)
    qn = qt * lax.rsqrt(qss * (1.0 / SB_HEAD_DIM) + EPS)
    qn = qn.reshape(D_SB, t) * (qg_ref[...] * (SB_HEAD_DIM ** -0.5))
    qt_ref[...] = qn.astype(BF16)

    vt = _nt_dot(wtv_ref[...], h).astype(BF16)
    for c in range(t // ATT_K):
        vt_ref[c] = vt[:, c * ATT_K:(c + 1) * ATT_K]


def _inproj(x2d, gmix, wnat, wtq, wtv, qg, kg, bd):
    s = x2d.shape[0]
    t = INPROJ_TOKENS
    n_nat = wnat.shape[1]
    const = lambda shape: pl.BlockSpec(shape, lambda i: (0,) * len(shape))
    return pl.pallas_call(
        _inproj_kernel,
        grid=(s // t,),
        in_specs=[
            pl.BlockSpec((t, D_MODEL), lambda i: (i, 0)),
            const((1, D_MODEL)),
            const((D_MODEL, n_nat)),
            const((D_SB, D_MODEL)),
            const((D_SB, D_MODEL)),
            const((D_SB, 1)),
            const((1, D_SB)),
            const((D_SB, D_SB)),
        ],
        out_specs=[
            pl.BlockSpec((D_SB, t), lambda i: (0, i)),
            pl.BlockSpec((t, D_SB), lambda i: (i, 0)),
            pl.BlockSpec((t // ATT_K, D_SB, ATT_K), lambda i: (i, 0, 0)),
            pl.BlockSpec((t, D_RG), lambda i: (i, 0)),
            pl.BlockSpec((t, D_RG), lambda i: (i, 0)),
        ],
        out_shape=[
            jax.ShapeDtypeStruct((D_SB, s), BF16),
            jax.ShapeDtypeStruct((s, D_SB), BF16),
            jax.ShapeDtypeStruct((s // ATT_K, D_SB, ATT_K), BF16),
            jax.ShapeDtypeStruct((s, D_RG), F32),
            jax.ShapeDtypeStruct((s, D_RG), F32),
        ],
        compiler_params=pltpu.CompilerParams(
            dimension_semantics=("arbitrary",), vmem_limit_bytes=VMEM_LIMIT_BYTES),
        name="inproj",
    )(x2d, gmix, wnat, wtq, wtv, qg, kg, bd)


def _attn_kernel(qt_ref, k_ref, vt_ref, tri_ref, ot_ref, d_ref, s_ref, w_ref):
    hd = pl.program_id(0)
    j = pl.program_id(1)
    q = qt_ref[...]
    zero = jnp.zeros_like(q)
    qpad = jnp.where(hd % 2 == 0,
                     jnp.concatenate([q, zero], axis=0),
                     jnp.concatenate([zero, q], axis=0))
    tri = tri_ref[...]
    subs = ATT_Q // ATT_K
    q0 = j * ATT_Q

    def a_mm(sb):
        out = []
        for c in range(subs):
            ks = pl.multiple_of((sb * subs + c) * ATT_K, ATT_K)
            out.append(jnp.dot(k_ref[pl.ds(ks, ATT_K), :], qpad, preferred_element_type=F32))
        return out

    def a_ew(zs, sb, slot, masked):
        for c, z in enumerate(zs):
            sp = jnp.maximum(z, 0.0) + jnp.log(1.0 + jnp.exp(-jnp.abs(z)))
            d = z - sp
            if masked:
                kpos = (sb * subs + c) * ATT_K + lax.broadcasted_iota(jnp.int32, z.shape, 0)
                qpos = q0 + lax.broadcasted_iota(jnp.int32, z.shape, 1)
                valid = kpos < qpos
                sp = jnp.where(valid, sp, 0.0)
                d = jnp.where(valid, d, NEG_INF)
            d_ref[slot, c] = d
            s_ref[slot, c] = sp.astype(BF16)

    def b_mm(slot):
        return [jnp.dot(tri, s_ref[slot, c], preferred_element_type=F32) for c in range(subs)]

    def b_ew(sufs, slot, carry):
        for c in range(subs - 1, -1, -1):
            w_ref[slot, c] = jnp.exp(d_ref[slot, c] - sufs[c] - carry).astype(BF16)
            carry = carry + (sufs[c][0:1, :] + s_ref[slot, c, 0:1, :].astype(F32))
        return carry

    def c_mm(sb, slot, acc):
        for c in range(subs):
            acc = acc + jnp.dot(vt_ref[sb * subs + c], w_ref[slot, c], preferred_element_type=F32)
        return acc

    def tick(n, slot, carry, acc, do_a=True, do_b=True, do_c=True):
        if do_a:
            zs = a_mm(j - n)
        if do_b:
            sufs = b_mm(1 - slot)
        if do_c:
            acc = c_mm(j - n + 2, slot, acc)
        if do_a:
            a_ew(zs, j - n, slot, False)
        if do_b:
            carry = b_ew(sufs, 1 - slot, carry)
        return carry, acc

    n_steps = j + 1
    carry = jnp.zeros((1, ATT_Q), F32)
    acc = jnp.zeros((SB_HEAD_DIM, ATT_Q), F32)
    a_ew(a_mm(j), j, 0, True)

    def single(ca):
        carry, acc = tick(1, 1, *ca, do_a=False, do_c=False)
        return tick(2, 0, carry, acc, do_a=False, do_b=False)

    def multi(ca):
        carry, acc = tick(1, 1, *ca, do_c=False)

        n_pairs = (n_steps - 2) // 2

        def cond(state):
            p, done, _, _ = state
            return jnp.logical_and(p < n_pairs, done == 0)

        def body(state):
            p, _, carry, acc = state
            carry, acc = tick(2 * p + 2, 0, carry, acc)
            done = (jnp.min(carry) >= ATT_ZERO_WEIGHT).astype(jnp.int32)
            carry, acc = tick(2 * p + 3, 1, carry, acc)
            return p + 1, done, carry, acc

        _, done, carry, acc = lax.while_loop(cond, body, (jnp.int32(0), jnp.int32(0), carry, acc))

        def tail_even(ca):
            carry, acc = tick(n_steps, 0, *ca, do_a=False)
            return tick(n_steps + 1, 1, carry, acc, do_a=False, do_b=False)

        def tail_odd(ca):
            carry, acc = tick(n_steps - 1, 0, *ca)
            carry, acc = tick(n_steps, 1, carry, acc, do_a=False)
            return tick(n_steps + 1, 0, carry, acc, do_a=False, do_b=False)

        def tails(ca):
            return lax.cond(n_steps % 2 == 0, tail_even, tail_odd, ca)

        return lax.cond(done == 1, lambda ca: ca, tails, (carry, acc))

    carry, acc = lax.cond(n_steps == 1, single, multi, (carry, acc))
    ot_ref[...] = acc


def _attention(qt, k, vt, tri):
    s = k.shape[0]
    return pl.pallas_call(
        _attn_kernel,
        grid=(SB_HEADS, s // ATT_Q),
        in_specs=[
            pl.BlockSpec((SB_HEAD_DIM, ATT_Q), lambda h, j: (h, j)),
            pl.BlockSpec((s, 2 * SB_HEAD_DIM), lambda h, j: (0, h // 2)),
            pl.BlockSpec((s // ATT_K, SB_HEAD_DIM, ATT_K), lambda h, j: (0, h, 0)),
            pl.BlockSpec((ATT_K, ATT_K), lambda h, j: (0, 0)),
        ],
        out_specs=pl.BlockSpec((SB_HEAD_DIM, ATT_Q), lambda h, j: (h, j)),
        out_shape=jax.ShapeDtypeStruct((D_SB, s), F32),
        scratch_shapes=[
            pltpu.VMEM((2, ATT_Q // ATT_K, ATT_K, ATT_Q), F32),
            pltpu.VMEM((2, ATT_Q // ATT_K, ATT_K, ATT_Q), BF16),
            pltpu.VMEM((2, ATT_Q // ATT_K, ATT_K, ATT_Q), BF16),
        ],
        compiler_params=pltpu.CompilerParams(
            dimension_semantics=("arbitrary", "arbitrary"), vmem_limit_bytes=VMEM_LIMIT_BYTES),
        name="sb_attention",
    )(qt, k, vt, tri)


def _mix_kernel(xrg_ref, grg_ref, ot_ref, x_ref, cw_ref, cb_ref, wa_ref, ba_ref, wx_ref, bx_ref,
                lam_ref, nsb_ref, nrg_ref, wout_ref, nffn_ref,
                x2_ref, h2t_ref, xs_ref, hs_ref, hstate_ref):
    t = xrg_ref.shape[0]
    hist = SUBLANES

    @pl.when(pl.program_id(0) == 0)
    def _():
        xs_ref[0:hist, :] = jnp.zeros((hist, D_RG), F32)
        hstate_ref[...] = jnp.zeros_like(hstate_ref)

    xs_ref[hist:hist + t, :] = xrg_ref[...]
    y = cb_ref[...] + cw_ref[0:1, :] * xs_ref[hist - 3:hist - 3 + t, :]
    for jj in range(1, CONV_WIDTH):
        y = y + cw_ref[jj:jj + 1, :] * xs_ref[hist - 3 + jj:hist - 3 + jj + t, :]
    xs_ref[0:hist, :] = xs_ref[t:t + hist, :]

    yb = y.astype(BF16)
    r = jax.nn.sigmoid(jnp.dot(yb, wa_ref[...], preferred_element_type=F32) + ba_ref[...])
    ig = jax.nn.sigmoid(jnp.dot(yb, wx_ref[...], preferred_element_type=F32) + bx_ref[...])
    log_a = (-RG_C) * r * _softplus(-lam_ref[...])
    a = jnp.exp(log_a)
    b = jnp.sqrt(jnp.tanh(-log_a) * (1.0 + a * a)) * (ig * y)

    rowmod = lax.broadcasted_iota(jnp.int32, (t, D_RG), 0) % SUBLANES
    d = 1
    while d < SUBLANES:
        keep = rowmod >= d
        a_sh = pltpu.roll(a, d, axis=0)
        b_sh = pltpu.roll(b, d, axis=0)
        b = jnp.where(keep, b + a * b_sh, b)
        a = jnp.where(keep, a * a_sh, a)
        d *= 2
    hprev = hstate_ref[...]
    for g in range(t // SUBLANES):
        sl = slice(g * SUBLANES, (g + 1) * SUBLANES)
        hg = a[sl] * hprev + b[sl]
        hs_ref[sl, :] = hg
        hprev = jnp.broadcast_to(hg[SUBLANES - 1:SUBLANES, :], (SUBLANES, D_RG))
    hstate_ref[...] = hprev

    o_rg = hs_ref[...] * _gelu_tanh(grg_ref[...])
    o_sb = ot_ref[...].T

    def rms(v, g):
        return v * lax.rsqrt(jnp.mean(v * v, axis=-1, keepdims=True) + EPS) * g

    n_sb = rms(o_sb, nsb_ref[...]).astype(BF16)
    n_rg = rms(o_rg, nrg_ref[...]).astype(BF16)
    mix = (jnp.dot(n_sb, wout_ref[0:D_SB, :], preferred_element_type=F32)
           + jnp.dot(n_rg, wout_ref[D_SB:, :], preferred_element_type=F32))
    x2 = x_ref[...] + mix
    x2_ref[...] = x2
    h2t_ref[...] = rms(x2, nffn_ref[...]).T.astype(BF16)


def _mix(xrg, grg, ot, x2d, cw, cb, wa, ba, wx, bx, lam, nsb, nrg, wout, nffn):
    s = x2d.shape[0]
    t = MIX_TOKENS
    const = lambda shape: pl.BlockSpec(shape, lambda i: (0,) * len(shape))
    return pl.pallas_call(
        _mix_kernel,
        grid=(s // t,),
        in_specs=[
            pl.BlockSpec((t, D_RG), lambda i: (i, 0)),
            pl.BlockSpec((t, D_RG), lambda i: (i, 0)),
            pl.BlockSpec((D_SB, t), lambda i: (0, i)),
            pl.BlockSpec((t, D_MODEL), lambda i: (i, 0)),
            const((CONV_WIDTH, D_RG)), const((1, D_RG)),
            const((D_RG, D_RG)), const((1, D_RG)),
            const((D_RG, D_RG)), const((1, D_RG)),
            const((1, D_RG)), const((1, D_SB)), const((1, D_RG)),
            const((D_MODEL, D_MODEL)), const((1, D_MODEL)),
        ],
        out_specs=[
            pl.BlockSpec((t, D_MODEL), lambda i: (i, 0)),
            pl.BlockSpec((D_MODEL, t), lambda i: (0, i)),
        ],
        out_shape=[
            jax.ShapeDtypeStruct((s, D_MODEL), F32),
            jax.ShapeDtypeStruct((D_MODEL, s), BF16),
        ],
        scratch_shapes=[
            pltpu.VMEM((t + SUBLANES, D_RG), F32),
            pltpu.VMEM((t, D_RG), F32),
            pltpu.VMEM((SUBLANES, D_RG), F32),
        ],
        compiler_params=pltpu.CompilerParams(
            dimension_semantics=("arbitrary",), vmem_limit_bytes=VMEM_LIMIT_BYTES),
        name="rglru_outproj",
    )(xrg, grg, ot, x2d, cw, cb, wa, ba, wx, bx, lam, nsb, nrg, wout, nffn)


_N_RANK = PEER_TOPK + 1
_CAND_PAIRS = [(a, b) for a in range(_N_RANK) for b in range(_N_RANK)
               if (a + 1) * (b + 1) <= _N_RANK]


def _sort_network(n):
    pairs = []
    p = 1
    while p < n:
        k = p
        while k >= 1:
            for j in range(k % p, n - k, 2 * k):
                for i in range(min(k, n - j - k)):
                    if (i + j) // (2 * p) == (i + j + k) // (2 * p):
                        pairs.append((i + j, i + j + k))
            k //= 2
        p *= 2
    return pairs


def _top_desc(s, n):
    out = []
    for r in range(n):
        m = jnp.max(s, axis=0, keepdims=True)
        out.append(m)
        if r + 1 < n:
            s = jnp.where(s == m, NEG_INF, s)
    return out


def _top_desc_sorted(s, n):
    groups = s.shape[0] // SUBLANES
    cols = [s[g * SUBLANES:(g + 1) * SUBLANES] for g in range(groups)]
    for a, b in _sort_network(groups):
        cols[a], cols[b] = jnp.maximum(cols[a], cols[b]), jnp.minimum(cols[a], cols[b])
    out = []
    for r in range(n):
        m = jnp.max(cols[0], axis=0, keepdims=True)
        out.append(m)
        remaining = n - 1 - r
        if remaining:
            hit = cols[0] == m
            for d in range(min(groups, remaining)):
                nxt = cols[d + 1] if d + 1 < groups else NEG_INF
                cols[d] = jnp.where(hit, nxt, cols[d])
    return out


def _peer_route(ht, wq_ref, keys_ref, e1_ref, th_ref, e2_ref, s2_ref):
    for hd in range(PEER_HEADS):
        qt = jnp.dot(wq_ref[hd * PEER_D_KEY:(hd + 1) * PEER_D_KEY, :], ht,
                     preferred_element_type=F32).astype(BF16)
        s1 = jnp.dot(keys_ref[hd, 0], qt[:PEER_HALF], preferred_element_type=F32)
        s2 = jnp.dot(keys_ref[hd, 1], qt[PEER_HALF:], preferred_element_type=F32)
        top1 = _top_desc_sorted(s1, _N_RANK)
        top2 = _top_desc_sorted(s2, _N_RANK)
        cand = jnp.concatenate([top1[a] + top2[b] for a, b in _CAND_PAIRS], axis=0)
        best = _top_desc(cand, _N_RANK)
        zsum = jnp.ones_like(best[0])
        for kk in range(1, PEER_TOPK):
            zsum = zsum + jnp.exp(best[kk] - best[0])
        tau = 0.5 * (best[PEER_TOPK - 1] + best[PEER_TOPK])
        e1_ref[hd] = jnp.exp(s1 - top1[0]) / zsum
        th_ref[hd] = tau - s1
        e2_ref[hd] = jnp.exp(s2 - top2[0])
        s2_ref[hd] = s2


def _peer_kernel(ht_ref, x2_ref, wq_ref, keys_ref, u_ref, vt_ref, out_ref,
                 e1_ref, th_ref, e2_ref, s2_ref, p_ref, acc_ref):
    e = pl.program_id(1)
    t = ht_ref.shape[1]
    n1 = PEER_EXPERTS // PEER_N_KEYS
    assert n1 == SUBLANES

    @pl.when(e == 0)
    def _():
        _peer_route(ht_ref[...], wq_ref, keys_ref, e1_ref, th_ref, e2_ref, s2_ref)
        acc_ref[...] = jnp.zeros_like(acc_ref)

    i1_base = pl.multiple_of(e * n1, SUBLANES)
    il_per_chunk = PEER_CHUNK // PEER_N_KEYS
    half = PEER_N_KEYS // 2
    n_chunks = PEER_EXPERTS // PEER_CHUNK
    chunk = lambda ch: slice(ch * PEER_CHUNK, (ch + 1) * PEER_CHUNK)
    pre_all = jnp.dot(u_ref[...], ht_ref[...], preferred_element_type=F32)
    for ch in range(n_chunks):
        pre = pre_all[chunk(ch)]
        for tc in range(t // LANES):
            ls = slice(tc * LANES, (tc + 1) * LANES)
            th = [th_ref[hd, pl.ds(i1_base, n1), ls] for hd in range(PEER_HEADS)]
            e1 = [e1_ref[hd, pl.ds(i1_base, n1), ls] for hd in range(PEER_HEADS)]
            act = _gelu_tanh(pre[:, ls])
            for hf in range(2):
                ks = slice(hf * half, (hf + 1) * half)
                w = [jnp.zeros((half, LANES), F32) for _ in range(il_per_chunk)]
                for hd in range(PEER_HEADS):
                    s2 = s2_ref[hd, ks, ls]
                    e2 = e2_ref[hd, ks, ls]
                    for ii in range(il_per_chunk):
                        il = ch * il_per_chunk + ii
                        w[ii] = w[ii] + jnp.where(s2 > th[hd][il:il + 1], e1[hd][il:il + 1] * e2, 0.0)
                for ii in range(il_per_chunk):
                    r0 = ii * PEER_N_KEYS + hf * half
                    p_ref[ch * PEER_CHUNK + r0:ch * PEER_CHUNK + r0 + half, ls] = (
                        w[ii] * act[r0:r0 + half]).astype(BF16)
    acc_ref[...] += jnp.dot(vt_ref[...], p_ref[...], preferred_element_type=F32)

    @pl.when(e == pl.num_programs(1) - 1)
    def _():
        out_ref[...] = x2_ref[...] + acc_ref[...].T


def _peer(h2t, x2, wq_t, keys, u_bf, vt_bf):
    s = h2t.shape[1]
    t = PEER_TOKENS
    n_exp = u_bf.shape[0]
    et = PEER_EXPERTS
    rt = lambda: pltpu.VMEM((PEER_HEADS, PEER_N_KEYS, t), F32)
    return pl.pallas_call(
        _peer_kernel,
        grid=(s // t, n_exp // et),
        in_specs=[
            pl.BlockSpec((D_MODEL, t), lambda i, e: (0, i)),
            pl.BlockSpec((t, D_MODEL), lambda i, e: (i, 0)),
            pl.BlockSpec((PEER_HEADS * PEER_D_KEY, D_MODEL), lambda i, e: (0, 0)),
            pl.BlockSpec((PEER_HEADS, 2, PEER_N_KEYS, PEER_HALF), lambda i, e: (0, 0, 0, 0)),
            pl.BlockSpec((et, D_MODEL), lambda i, e: (e, 0)),
            pl.BlockSpec((D_MODEL, et), lambda i, e: (0, e)),
        ],
        out_specs=pl.BlockSpec((t, D_MODEL), lambda i, e: (i, 0)),
        out_shape=jax.ShapeDtypeStruct((s, D_MODEL), F32),
        scratch_shapes=[rt(), rt(), rt(), rt(),
                        pltpu.VMEM((et, t), BF16),
                        pltpu.VMEM((D_MODEL, t), F32)],
        compiler_params=pltpu.CompilerParams(
            dimension_semantics=("arbitrary", "arbitrary"), vmem_limit_bytes=VMEM_LIMIT_BYTES),
        name="peer",
    )(h2t, x2, wq_t, keys, u_bf, vt_bf)


def _block_diag(w):
    n, bi, bj = w.shape
    eye = jnp.eye(n, dtype=w.dtype)
    return (eye[:, None, :, None] * w[:, :, None, :]).reshape(n * bi, n * bj)


def kernel(x, norm_mix, w_in, q_norm, k_norm, conv_w, conv_b, rg_w_a, rg_b_a, rg_w_x, rg_b_x,
           rg_lambda, out_norm_sb, out_norm_rg, w_out, norm_ffn, peer_w_query, peer_sub_keys,
           peer_u, peer_v):
    bsz, s, d = x.shape
    assert bsz == 1 and d == D_MODEL
    assert s % max(INPROJ_TOKENS, ATT_Q, MIX_TOKENS, PEER_TOKENS) == 0
    depth = w_in.shape[0]
    x2d = x.reshape(s, d)

    head_id = jnp.arange(D_SB) // SB_HEAD_DIM
    bd = (head_id[:, None] == head_id[None, :]).astype(BF16)
    kidx = jnp.arange(ATT_K)
    tri = (kidx[None, :] > kidx[:, None]).astype(BF16)
    row = lambda v: v.reshape(1, -1)

    for l in range(depth):
        w = w_in[l]
        wtq = w[:, 0:D_SB].T.astype(BF16)
        wtv = w[:, 2 * D_SB:3 * D_SB].T.astype(BF16)
        wnat = jnp.concatenate([w[:, D_SB:2 * D_SB], w[:, 3 * D_SB:]], axis=1).astype(BF16)
        qg = jnp.tile(q_norm[l], SB_HEADS).reshape(D_SB, 1)
        kg = jnp.tile(k_norm[l], SB_HEADS).reshape(1, D_SB)
        qt, k, vt, xrg, grg = _inproj(x2d, row(norm_mix[l]), wnat, wtq, wtv, qg, kg, bd)

        ot = _attention(qt, k, vt, tri)

        x2, h2t = _mix(xrg, grg, ot, x2d, conv_w[l], row(conv_b[l]),
                       _block_diag(rg_w_a[l]).astype(BF16), row(rg_b_a[l]),
                       _block_diag(rg_w_x[l]).astype(BF16), row(rg_b_x[l]),
                       row(rg_lambda[l]), row(out_norm_sb[l]), row(out_norm_rg[l]),
                       w_out[l].astype(BF16), row(norm_ffn[l]))

        wq_t = peer_w_query[l].reshape(d, PEER_HEADS * PEER_D_KEY).T.astype(BF16)
        x2d = _peer(h2t, x2, wq_t, peer_sub_keys[l].astype(BF16),
                    peer_u[l].astype(BF16), peer_v[l].T.astype(BF16))
    return x2d.reshape(bsz, s, d)
```

```python
import jax
import jax.numpy as jnp
from jax import lax
from jax.experimental import pallas as pl
from jax.experimental.pallas import tpu as pltpu

F32 = jnp.float32
BF16 = jnp.bfloat16

D_MODEL = 1024
SB_HEADS = 8
SB_HEAD_DIM = 64
D_SB = SB_HEADS * SB_HEAD_DIM
D_RG = D_MODEL - D_SB
RG_BLOCKS = 8
RG_BLOCK_DIM = D_RG // RG_BLOCKS
CONV_WIDTH = 4
RG_C = 8.0
PEER_HEADS = 8
PEER_N_KEYS = 128
PEER_D_KEY = 256
PEER_HALF = PEER_D_KEY // 2
PEER_TOPK = 16
EPS = 1e-6

VMEM_LIMIT_BYTES = 56 * 1024 * 1024
SUBLANES = 8
LANES = 128

INPROJ_TOKENS = 1024
ATT_Q = 256
ATT_K = 256
MIX_TOKENS = 512
PEER_TOKENS = 512
PEER_EXPERTS = 1024
PEER_CHUNK = 256

NEG_INF = float("-inf")
ATT_ZERO_WEIGHT = 110.0


def _nt_dot(a, b):
    return lax.dot_general(a, b, (((1,), (1,)), ((), ())), preferred_element_type=F32)


def _softplus(z):
    return jnp.maximum(z, 0.0) + jnp.log(1.0 + jnp.exp(-jnp.abs(z)))


def _gelu_tanh(x):
    c0 = 0.7978845608028654
    c1 = 0.044715 * c0
    half = 0.5 * x
    return half + half * jnp.tanh(x * (c1 * (x * x) + c0))


def _inproj_kernel(x_ref, gmix_ref, wnat_ref, wtq_ref, wtv_ref, qg_ref, kg_ref, bd_ref,
                   qt_ref, k_ref, vt_ref, xrg_ref, grg_ref):
    x = x_ref[...]
    ms = jnp.mean(x * x, axis=-1, keepdims=True)
    h = (x * lax.rsqrt(ms + EPS) * gmix_ref[...]).astype(BF16)

    nat = jnp.dot(h, wnat_ref[...], preferred_element_type=F32)
    k = nat[:, :D_SB]
    kk = k * k
    kk_hi = kk.astype(BF16)
    kk_lo = (kk - kk_hi.astype(F32)).astype(BF16)
    kss = (jnp.dot(kk_hi, bd_ref[...], preferred_element_type=F32)
           + jnp.dot(kk_lo, bd_ref[...], preferred_element_type=F32))
    k_ref[...] = (k * lax.rsqrt(kss * (1.0 / SB_HEAD_DIM) + EPS) * kg_ref[...]).astype(BF16)
    xrg_ref[...] = nat[:, D_SB:D_SB + D_RG]
    grg_ref[...] = nat[:, D_SB + D_RG:]

    t = x.shape[0]
    qt = _nt_dot(wtq_ref[...], h).reshape(SB_HEADS, SB_HEAD_DIM, t)
    qss = jnp.sum(qt * qt, axis=1, keepdims=True)
    qn = qt * lax.rsqrt(qss * (1.0 / SB_HEAD_DIM) + EPS)
    qn = qn.reshape(D_SB, t) * (qg_ref[...] * (SB_HEAD_DIM ** -0.5))
    qt_ref[...] = qn.astype(BF16)

    vt = _nt_dot(wtv_ref[...], h).astype(BF16)
    for c in range(t // ATT_K):
        vt_ref[c] = vt[:, c * ATT_K:(c + 1) * ATT_K]


def _inproj(x2d, gmix, wnat, wtq, wtv, qg, kg, bd):
    s = x2d.shape[0]
    t = INPROJ_TOKENS
    n_nat = wnat.shape[1]
    const = lambda shape: pl.BlockSpec(shape, lambda i: (0,) * len(shape))
    return pl.pallas_call(
        _inproj_kernel,
        grid=(s // t,),
        in_specs=[
            pl.BlockSpec((t, D_MODEL), lambda i: (i, 0)),
            const((1, D_MODEL)),
            const((D_MODEL, n_nat)),
            const((D_SB, D_MODEL)),
            const((D_SB, D_MODEL)),
            const((D_SB, 1)),
            const((1, D_SB)),
            const((D_SB, D_SB)),
        ],
        out_specs=[
            pl.BlockSpec((D_SB, t), lambda i: (0, i)),
            pl.BlockSpec((t, D_SB), lambda i: (i, 0)),
            pl.BlockSpec((t // ATT_K, D_SB, ATT_K), lambda i: (i, 0, 0)),
            pl.BlockSpec((t, D_RG), lambda i: (i, 0)),
            pl.BlockSpec((t, D_RG), lambda i: (i, 0)),
        ],
        out_shape=[
            jax.ShapeDtypeStruct((D_SB, s), BF16),
            jax.ShapeDtypeStruct((s, D_SB), BF16),
            jax.ShapeDtypeStruct((s // ATT_K, D_SB, ATT_K), BF16),
            jax.ShapeDtypeStruct((s, D_RG), F32),
            jax.ShapeDtypeStruct((s, D_RG), F32),
        ],
        compiler_params=pltpu.CompilerParams(
            dimension_semantics=("arbitrary",), vmem_limit_bytes=VMEM_LIMIT_BYTES),
        name="inproj",
    )(x2d, gmix, wnat, wtq, wtv, qg, kg, bd)


def _attn_kernel(qt_ref, k_ref, vt_ref, tri_ref, ot_ref, d_ref, s_ref, w_ref):
    j = pl.program_id(1)
    tri = tri_ref[...]
    subs = ATT_Q // ATT_K
    q0 = j * ATT_Q
    for hh in range(2):
        _attn_head(hh, j, q0, subs, tri, qt_ref, k_ref, vt_ref, ot_ref, d_ref, s_ref, w_ref)


def _attn_head(hh, j, q0, subs, tri, qt_ref, k_ref, vt_ref, ot_ref, d_ref, s_ref, w_ref):
    rows = slice(hh * SB_HEAD_DIM, (hh + 1) * SB_HEAD_DIM)
    q = qt_ref[rows, :]
    zero = jnp.zeros_like(q)
    qpad = jnp.concatenate([q, zero] if hh == 0 else [zero, q], axis=0)

    def a_mm(sb):
        out = []
        for c in range(subs):
            ks = pl.multiple_of((sb * subs + c) * ATT_K, ATT_K)
            out.append(jnp.dot(k_ref[pl.ds(ks, ATT_K), :], qpad, preferred_element_type=F32))
        return out

    def a_ew(zs, sb, slot, masked):
        for c, z in enumerate(zs):
            sp = jnp.maximum(z, 0.0) + jnp.log(1.0 + jnp.exp(-jnp.abs(z)))
            d = z - sp
            if masked:
                kpos = (sb * subs + c) * ATT_K + lax.broadcasted_iota(jnp.int32, z.shape, 0)
                qpos = q0 + lax.broadcasted_iota(jnp.int32, z.shape, 1)
                valid = kpos < qpos
                sp = jnp.where(valid, sp, 0.0)
                d = jnp.where(valid, d, NEG_INF)
            d_ref[slot, c] = d
            s_ref[slot, c] = sp.astype(BF16)

    def b_mm(slot):
        return [jnp.dot(tri, s_ref[slot, c], preferred_element_type=F32) for c in range(subs)]

    def b_ew(sufs, slot, carry):
        for c in range(subs - 1, -1, -1):
            w_ref[slot, c] = jnp.exp(d_ref[slot, c] - sufs[c] - carry).astype(BF16)
            carry = carry + (sufs[c][0:1, :] + s_ref[slot, c, 0:1, :].astype(F32))
        return carry

    def c_mm(sb, slot, acc):
        for c in range(subs):
            acc = acc + jnp.dot(vt_ref[sb * subs + c, rows, :], w_ref[slot, c],
                                preferred_element_type=F32)
        return acc

    def tick(n, slot, carry, acc, do_a=True, do_b=True, do_c=True):
        if do_a:
            zs = a_mm(j - n)
        if do_b:
            sufs = b_mm(1 - slot)
        if do_c:
            acc = c_mm(j - n + 2, slot, acc)
        if do_a:
            a_ew(zs, j - n, slot, False)
        if do_b:
            carry = b_ew(sufs, 1 - slot, carry)
        return carry, acc

    n_steps = j + 1
    carry = jnp.zeros((1, ATT_Q), F32)
    acc = jnp.zeros((SB_HEAD_DIM, ATT_Q), F32)
    a_ew(a_mm(j), j, 0, True)

    def single(ca):
        carry, acc = tick(1, 1, *ca, do_a=False, do_c=False)
        return tick(2, 0, carry, acc, do_a=False, do_b=False)

    def multi(ca):
        carry, acc = tick(1, 1, *ca, do_c=False)

        n_pairs = (n_steps - 2) // 2

        def cond(state):
            p, done, _, _ = state
            return jnp.logical_and(p < n_pairs, done == 0)

        def body(state):
            p, _, carry, acc = state
            carry, acc = tick(2 * p + 2, 0, carry, acc)
            done = (jnp.min(carry) >= ATT_ZERO_WEIGHT).astype(jnp.int32)
            carry, acc = tick(2 * p + 3, 1, carry, acc)
            return p + 1, done, carry, acc

        _, done, carry, acc = lax.while_loop(cond, body, (jnp.int32(0), jnp.int32(0), carry, acc))

        def tail_even(ca):
            carry, acc = tick(n_steps, 0, *ca, do_a=False)
            return tick(n_steps + 1, 1, carry, acc, do_a=False, do_b=False)

        def tail_odd(ca):
            carry, acc = tick(n_steps - 1, 0, *ca)
            carry, acc = tick(n_steps, 1, carry, acc, do_a=False)
            return tick(n_steps + 1, 0, carry, acc, do_a=False, do_b=False)

        def tails(ca):
            return lax.cond(n_steps % 2 == 0, tail_even, tail_odd, ca)

        return lax.cond(done == 1, lambda ca: ca, tails, (carry, acc))

    carry, acc = lax.cond(n_steps == 1, single, multi, (carry, acc))
    ot_ref[rows, :] = acc


def _attention(qt, k, vt, tri):
    s = k.shape[0]
    return pl.pallas_call(
        _attn_kernel,
        grid=(SB_HEADS // 2, s // ATT_Q),
        in_specs=[
            pl.BlockSpec((2 * SB_HEAD_DIM, ATT_Q), lambda h, j: (h, j)),
            pl.BlockSpec((s, 2 * SB_HEAD_DIM), lambda h, j: (0, h)),
            pl.BlockSpec((s // ATT_K, 2 * SB_HEAD_DIM, ATT_K), lambda h, j: (0, h, 0)),
            pl.BlockSpec((ATT_K, ATT_K), lambda h, j: (0, 0)),
        ],
        out_specs=pl.BlockSpec((2 * SB_HEAD_DIM, ATT_Q), lambda h, j: (h, j)),
        out_shape=jax.ShapeDtypeStruct((D_SB, s), F32),
        scratch_shapes=[
            pltpu.VMEM((2, ATT_Q // ATT_K, ATT_K, ATT_Q), F32),
            pltpu.VMEM((2, ATT_Q // ATT_K, ATT_K, ATT_Q), BF16),
            pltpu.VMEM((2, ATT_Q // ATT_K, ATT_K, ATT_Q), BF16),
        ],
        compiler_params=pltpu.CompilerParams(
            dimension_semantics=("arbitrary", "arbitrary"), vmem_limit_bytes=VMEM_LIMIT_BYTES),
        name="sb_attention",
    )(qt, k, vt, tri)


def _mix_kernel(xrg_ref, grg_ref, ot_ref, x_ref, cw_ref, cb_ref, wa_ref, ba_ref, wx_ref, bx_ref,
                lam_ref, nsb_ref, nrg_ref, wout_ref, nffn_ref,
                x2_ref, h2t_ref, xs_ref, hs_ref, hstate_ref):
    t = xrg_ref.shape[0]
    hist = SUBLANES

    @pl.when(pl.program_id(0) == 0)
    def _():
        xs_ref[0:hist, :] = jnp.zeros((hist, D_RG), F32)
        hstate_ref[...] = jnp.zeros_like(hstate_ref)

    xs_ref[hist:hist + t, :] = xrg_ref[...]
    y = cb_ref[...] + cw_ref[0:1, :] * xs_ref[hist - 3:hist - 3 + t, :]
    for jj in range(1, CONV_WIDTH):
        y = y + cw_ref[jj:jj + 1, :] * xs_ref[hist - 3 + jj:hist - 3 + jj + t, :]
    xs_ref[0:hist, :] = xs_ref[t:t + hist, :]

    yb = y.astype(BF16)
    r = jax.nn.sigmoid(jnp.dot(yb, wa_ref[...], preferred_element_type=F32) + ba_ref[...])
    ig = jax.nn.sigmoid(jnp.dot(yb, wx_ref[...], preferred_element_type=F32) + bx_ref[...])
    log_a = (-RG_C) * r * _softplus(-lam_ref[...])
    a = jnp.exp(log_a)
    b = jnp.sqrt(jnp.tanh(-log_a) * (1.0 + a * a)) * (ig * y)

    rowmod = lax.broadcasted_iota(jnp.int32, (t, D_RG), 0) % SUBLANES
    d = 1
    while d < SUBLANES:
        keep = rowmod >= d
        a_sh = pltpu.roll(a, d, axis=0)
        b_sh = pltpu.roll(b, d, axis=0)
        b = jnp.where(keep, b + a * b_sh, b)
        a = jnp.where(keep, a * a_sh, a)
        d *= 2
    hprev = hstate_ref[...]
    for g in range(t // SUBLANES):
        sl = slice(g * SUBLANES, (g + 1) * SUBLANES)
        hg = a[sl] * hprev + b[sl]
        hs_ref[sl, :] = hg
        hprev = jnp.broadcast_to(hg[SUBLANES - 1:SUBLANES, :], (SUBLANES, D_RG))
    hstate_ref[...] = hprev

    o_rg = hs_ref[...] * _gelu_tanh(grg_ref[...])
    o_sb = ot_ref[...].T

    def rms(v, g):
        return v * lax.rsqrt(jnp.mean(v * v, axis=-1, keepdims=True) + EPS) * g

    n_sb = rms(o_sb, nsb_ref[...]).astype(BF16)
    n_rg = rms(o_rg, nrg_ref[...]).astype(BF16)
    mix = (jnp.dot(n_sb, wout_ref[0:D_SB, :], preferred_element_type=F32)
           + jnp.dot(n_rg, wout_ref[D_SB:, :], preferred_element_type=F32))
    x2 = x_ref[...] + mix
    x2_ref[...] = x2
    h2t_ref[...] = rms(x2, nffn_ref[...]).T.astype(BF16)


def _mix(xrg, grg, ot, x2d, cw, cb, wa, ba, wx, bx, lam, nsb, nrg, wout, nffn):
    s = x2d.shape[0]
    t = MIX_TOKENS
    const = lambda shape: pl.BlockSpec(shape, lambda i: (0,) * len(shape))
    return pl.pallas_call(
        _mix_kernel,
        grid=(s // t,),
        in_specs=[
            pl.BlockSpec((t, D_RG), lambda i: (i, 0)),
            pl.BlockSpec((t, D_RG), lambda i: (i, 0)),
            pl.BlockSpec((D_SB, t), lambda i: (0, i)),
            pl.BlockSpec((t, D_MODEL), lambda i: (i, 0)),
            const((CONV_WIDTH, D_RG)), const((1, D_RG)),
            const((D_RG, D_RG)), const((1, D_RG)),
            const((D_RG, D_RG)), const((1, D_RG)),
            const((1, D_RG)), const((1, D_SB)), const((1, D_RG)),
            const((D_MODEL, D_MODEL)), const((1, D_MODEL)),
        ],
        out_specs=[
            pl.BlockSpec((t, D_MODEL), lambda i: (i, 0)),
            pl.BlockSpec((D_MODEL, t), lambda i: (0, i)),
        ],
        out_shape=[
            jax.ShapeDtypeStruct((s, D_MODEL), F32),
            jax.ShapeDtypeStruct((D_MODEL, s), BF16),
        ],
        scratch_shapes=[
            pltpu.VMEM((t + SUBLANES, D_RG), F32),
            pltpu.VMEM((t, D_RG), F32),
            pltpu.VMEM((SUBLANES, D_RG), F32),
        ],
        compiler_params=pltpu.CompilerParams(
            dimension_semantics=("arbitrary",), vmem_limit_bytes=VMEM_LIMIT_BYTES),
        name="rglru_outproj",
    )(xrg, grg, ot, x2d, cw, cb, wa, ba, wx, bx, lam, nsb, nrg, wout, nffn)


_N_RANK = PEER_TOPK + 1
_CAND_PAIRS = [(a, b) for a in range(_N_RANK) for b in range(_N_RANK)
               if (a + 1) * (b + 1) <= _N_RANK]
_CAND_ROWS = 64


def _sort_network(n):
    pairs = []
    p = 1
    while p < n:
        k = p
        while k >= 1:
            for j in range(k % p, n - k, 2 * k):
                for i in range(min(k, n - j - k)):
                    if (i + j) // (2 * p) == (i + j + k) // (2 * p):
                        pairs.append((i + j, i + j + k))
            k //= 2
        p *= 2
    return pairs


def _top_desc(s, n):
    out = []
    for r in range(n):
        m = jnp.max(s, axis=0, keepdims=True)
        out.append(m)
        if r + 1 < n:
            s = jnp.where(s == m, NEG_INF, s)
    return out


def _top_desc_sorted(s, n):
    groups = s.shape[0] // SUBLANES
    cols = [s[g * SUBLANES:(g + 1) * SUBLANES] for g in range(groups)]
    for a, b in _sort_network(groups):
        cols[a], cols[b] = jnp.maximum(cols[a], cols[b]), jnp.minimum(cols[a], cols[b])
    out = []
    for r in range(n):
        m = jnp.max(cols[0], axis=0, keepdims=True)
        out.append(m)
        remaining = n - 1 - r
        if remaining:
            hit = cols[0] == m
            for d in range(min(groups, remaining)):
                nxt = cols[d + 1] if d + 1 < groups else NEG_INF
                cols[d] = jnp.where(hit, nxt, cols[d])
    return out


def _peer_route(ht, wq_ref, keys_ref, e1_ref, th_ref, e2_ref, s2_ref):
    for hd in range(PEER_HEADS):
        qt = jnp.dot(wq_ref[hd * PEER_D_KEY:(hd + 1) * PEER_D_KEY, :], ht,
                     preferred_element_type=F32).astype(BF16)
        s1 = jnp.dot(keys_ref[hd, 0], qt[:PEER_HALF], preferred_element_type=F32)
        s2 = jnp.dot(keys_ref[hd, 1], qt[PEER_HALF:], preferred_element_type=F32)
        top1 = _top_desc_sorted(s1, _N_RANK)
        top2 = _top_desc_sorted(s2, _N_RANK)
        cand = [top1[a] + top2[b] for a, b in _CAND_PAIRS]
        cand += [jnp.full_like(cand[0], NEG_INF)] * (_CAND_ROWS - len(cand))
        best = _top_desc_sorted(jnp.concatenate(cand, axis=0), _N_RANK)
        zsum = jnp.ones_like(best[0])
        for kk in range(1, PEER_TOPK):
            zsum = zsum + jnp.exp(best[kk] - best[0])
        tau = 0.5 * (best[PEER_TOPK - 1] + best[PEER_TOPK])
        e1_ref[hd] = jnp.exp(s1 - top1[0]) / zsum
        th_ref[hd] = tau - s1
        e2_ref[hd] = jnp.exp(s2 - top2[0])
        s2_ref[hd] = s2


def _peer_kernel(ht_ref, x2_ref, wq_ref, keys_ref, u_ref, vt_ref, out_ref,
                 e1_ref, th_ref, e2_ref, s2_ref, p_ref, acc_ref):
    e = pl.program_id(1)
    t = ht_ref.shape[1]
    n1 = PEER_EXPERTS // PEER_N_KEYS
    assert n1 == SUBLANES

    @pl.when(e == 0)
    def _():
        _peer_route(ht_ref[...], wq_ref, keys_ref, e1_ref, th_ref, e2_ref, s2_ref)
        acc_ref[...] = jnp.zeros_like(acc_ref)

    i1_base = pl.multiple_of(e * n1, SUBLANES)
    il_per_chunk = PEER_CHUNK // PEER_N_KEYS
    half = PEER_N_KEYS // 2
    n_chunks = PEER_EXPERTS // PEER_CHUNK
    chunk = lambda ch: slice(ch * PEER_CHUNK, (ch + 1) * PEER_CHUNK)
    pre_all = jnp.dot(u_ref[...], ht_ref[...], preferred_element_type=F32)
    for ch in range(n_chunks):
        pre = pre_all[chunk(ch)]
        for tc in range(t // LANES):
            ls = slice(tc * LANES, (tc + 1) * LANES)
            th = [th_ref[hd, pl.ds(i1_base, n1), ls] for hd in range(PEER_HEADS)]
            e1 = [e1_ref[hd, pl.ds(i1_base, n1), ls] for hd in range(PEER_HEADS)]
            act = _gelu_tanh(pre[:, ls])
            for hf in range(2):
                ks = slice(hf * half, (hf + 1) * half)
                w = [jnp.zeros((half, LANES), F32) for _ in range(il_per_chunk)]
                for hd in range(PEER_HEADS):
                    s2 = s2_ref[hd, ks, ls]
                    e2 = e2_ref[hd, ks, ls]
                    for ii in range(il_per_chunk):
                        il = ch * il_per_chunk + ii
                        w[ii] = w[ii] + jnp.where(s2 > th[hd][il:il + 1], e1[hd][il:il + 1] * e2, 0.0)
                for ii in range(il_per_chunk):
                    r0 = ii * PEER_N_KEYS + hf * half
                    p_ref[ch * PEER_CHUNK + r0:ch * PEER_CHUNK + r0 + half, ls] = (
                        w[ii] * act[r0:r0 + half]).astype(BF16)
    acc_ref[...] += jnp.dot(vt_ref[...], p_ref[...], preferred_element_type=F32)

    @pl.when(e == pl.num_programs(1) - 1)
    def _():
        out_ref[...] = x2_ref[...] + acc_ref[...].T


def _peer(h2t, x2, wq_t, keys, u_bf, vt_bf):
    s = h2t.shape[1]
    t = PEER_TOKENS
    n_exp = u_bf.shape[0]
    et = PEER_EXPERTS
    rt = lambda: pltpu.VMEM((PEER_HEADS, PEER_N_KEYS, t), F32)
    return pl.pallas_call(
        _peer_kernel,
        grid=(s // t, n_exp // et),
        in_specs=[
            pl.BlockSpec((D_MODEL, t), lambda i, e: (0, i)),
            pl.BlockSpec((t, D_MODEL), lambda i, e: (i, 0)),
            pl.BlockSpec((PEER_HEADS * PEER_D_KEY, D_MODEL), lambda i, e: (0, 0)),
            pl.BlockSpec((PEER_HEADS, 2, PEER_N_KEYS, PEER_HALF), lambda i, e: (0, 0, 0, 0)),
            pl.BlockSpec((et, D_MODEL), lambda i, e: (e, 0)),
            pl.BlockSpec((D_MODEL, et), lambda i, e: (0, e)),
        ],
        out_specs=pl.BlockSpec((t, D_MODEL), lambda i, e: (i, 0)),
        out_shape=jax.ShapeDtypeStruct((s, D_MODEL), F32),
        scratch_shapes=[rt(), rt(), rt(), rt(),
                        pltpu.VMEM((et, t), BF16),
                        pltpu.VMEM((D_MODEL, t), F32)],
        compiler_params=pltpu.CompilerParams(
            dimension_semantics=("arbitrary", "arbitrary"), vmem_limit_bytes=VMEM_LIMIT_BYTES),
        name="peer",
    )(h2t, x2, wq_t, keys, u_bf, vt_bf)


def _block_diag(w):
    n, bi, bj = w.shape
    eye = jnp.eye(n, dtype=w.dtype)
    return (eye[:, None, :, None] * w[:, :, None, :]).reshape(n * bi, n * bj)


def kernel(x, norm_mix, w_in, q_norm, k_norm, conv_w, conv_b, rg_w_a, rg_b_a, rg_w_x, rg_b_x,
           rg_lambda, out_norm_sb, out_norm_rg, w_out, norm_ffn, peer_w_query, peer_sub_keys,
           peer_u, peer_v):
    bsz, s, d = x.shape
    assert bsz == 1 and d == D_MODEL
    assert s % max(INPROJ_TOKENS, ATT_Q, MIX_TOKENS, PEER_TOKENS) == 0
    depth = w_in.shape[0]
    x2d = x.reshape(s, d)

    head_id = jnp.arange(D_SB) // SB_HEAD_DIM
    bd = (head_id[:, None] == head_id[None, :]).astype(BF16)
    kidx = jnp.arange(ATT_K)
    tri = (kidx[None, :] > kidx[:, None]).astype(BF16)
    row = lambda v: v.reshape(1, -1)

    for l in range(depth):
        w = w_in[l]
        wtq = w[:, 0:D_SB].T.astype(BF16)
        wtv = w[:, 2 * D_SB:3 * D_SB].T.astype(BF16)
        wnat = jnp.concatenate([w[:, D_SB:2 * D_SB], w[:, 3 * D_SB:]], axis=1).astype(BF16)
        qg = jnp.tile(q_norm[l], SB_HEADS).reshape(D_SB, 1)
        kg = jnp.tile(k_norm[l], SB_HEADS).reshape(1, D_SB)
        qt, k, vt, xrg, grg = _inproj(x2d, row(norm_mix[l]), wnat, wtq, wtv, qg, kg, bd)

        ot = _attention(qt, k, vt, tri)

        x2, h2t = _mix(xrg, grg, ot, x2d, conv_w[l], row(conv_b[l]),
                       _block_diag(rg_w_a[l]).astype(BF16), row(rg_b_a[l]),
                       _block_diag(rg_w_x[l]).astype(BF16), row(rg_b_x[l]),
                       row(rg_lambda[l]), row(out_norm_sb[l]), row(out_norm_rg[l]),
                       w_out[l].astype(BF16), row(norm_ffn[l]))

        wq_t = peer_w_query[l].reshape(d, PEER_HEADS * PEER_D_KEY).T.astype(BF16)
        x2d = _peer(h2t, x2, wq_t, peer_sub_keys[l].astype(BF16),
                    peer_u[l].astype(BF16), peer_v[l].T.astype(BF16))
    return x2d.reshape(bsz, s, d)
```

```python
import jax
import jax.numpy as jnp
from jax import lax
from jax.experimental import pallas as pl
from jax.experimental.pallas import tpu as pltpu

F32 = jnp.float32
BF16 = jnp.bfloat16

D_MODEL = 1024
SB_HEADS = 8
SB_HEAD_DIM = 64
D_SB = SB_HEADS * SB_HEAD_DIM
D_RG = D_MODEL - D_SB
RG_BLOCKS = 8
RG_BLOCK_DIM = D_RG // RG_BLOCKS
CONV_WIDTH = 4
RG_C = 8.0
PEER_HEADS = 8
PEER_N_KEYS = 128
PEER_D_KEY = 256
PEER_HALF = PEER_D_KEY // 2
PEER_TOPK = 16
EPS = 1e-6

VMEM_LIMIT_BYTES = 56 * 1024 * 1024
SUBLANES = 8
LANES = 128

INPROJ_TOKENS = 1024
ATT_Q = 256
ATT_K = 256
MIX_TOKENS = 512
PEER_TOKENS = 512
PEER_EXPERTS = 1024
PEER_CHUNK = 256

NEG_INF = float("-inf")
ATT_ZERO_WEIGHT = 110.0


def _nt_dot(a, b):
    return lax.dot_general(a, b, (((1,), (1,)), ((), ())), preferred_element_type=F32)


def _softplus(z):
    return jnp.maximum(z, 0.0) + jnp.log(1.0 + jnp.exp(-jnp.abs(z)))


def _gelu_tanh(x):
    c0 = 0.7978845608028654
    c1 = 0.044715 * c0
    half = 0.5 * x
    return half + half * jnp.tanh(x * (c1 * (x * x) + c0))


def _inproj_kernel(x_ref, gmix_ref, wnat_ref, wtq_ref, wtv_ref, qg_ref, kg_ref, bd_ref,
                   qt_ref, k_ref, vt_ref, xrg_ref, grg_ref):
    x = x_ref[...]
    ms = jnp.mean(x * x, axis=-1, keepdims=True)
    h = (x * lax.rsqrt(ms + EPS) * gmix_ref[...]).astype(BF16)

    nat = jnp.dot(h, wnat_ref[...], preferred_element_type=F32)
    k = nat[:, :D_SB]
    kk = k * k
    kk_hi = kk.astype(BF16)
    kk_lo = (kk - kk_hi.astype(F32)).astype(BF16)
    kss = (jnp.dot(kk_hi, bd_ref[...], preferred_element_type=F32)
           + jnp.dot(kk_lo, bd_ref[...], preferred_element_type=F32))
    k_ref[...] = (k * lax.rsqrt(kss * (1.0 / SB_HEAD_DIM) + EPS) * kg_ref[...]).astype(BF16)
    xrg_ref[...] = nat[:, D_SB:D_SB + D_RG]
    grg_ref[...] = nat[:, D_SB + D_RG:]

    t = x.shape[0]
    qt = _nt_dot(wtq_ref[...], h).reshape(SB_HEADS, SB_HEAD_DIM, t)
    qss = jnp.sum(qt * qt, axis=1, keepdims=True)
    qn = qt * lax.rsqrt(qss * (1.0 / SB_HEAD_DIM) + EPS)
    qn = qn.reshape(D_SB, t) * (qg_ref[...] * (SB_HEAD_DIM ** -0.5))
    qt_ref[...] = qn.astype(BF16)

    vt = _nt_dot(wtv_ref[...], h).astype(BF16)
    for c in range(t // ATT_K):
        vt_ref[c] = vt[:, c * ATT_K:(c + 1) * ATT_K]


def _inproj(x2d, gmix, wnat, wtq, wtv, qg, kg, bd):
    s = x2d.shape[0]
    t = INPROJ_TOKENS
    n_nat = wnat.shape[1]
    const = lambda shape: pl.BlockSpec(shape, lambda i: (0,) * len(shape))
    return pl.pallas_call(
        _inproj_kernel,
        grid=(s // t,),
        in_specs=[
            pl.BlockSpec((t, D_MODEL), lambda i: (i, 0)),
            const((1, D_MODEL)),
            const((D_MODEL, n_nat)),
            const((D_SB, D_MODEL)),
            const((D_SB, D_MODEL)),
            const((D_SB, 1)),
            const((1, D_SB)),
            const((D_SB, D_SB)),
        ],
        out_specs=[
            pl.BlockSpec((D_SB, t), lambda i: (0, i)),
            pl.BlockSpec((t, D_SB), lambda i: (i, 0)),
            pl.BlockSpec((t // ATT_K, D_SB, ATT_K), lambda i: (i, 0, 0)),
            pl.BlockSpec((t, D_RG), lambda i: (i, 0)),
            pl.BlockSpec((t, D_RG), lambda i: (i, 0)),
        ],
        out_shape=[
            jax.ShapeDtypeStruct((D_SB, s), BF16),
            jax.ShapeDtypeStruct((s, D_SB), BF16),
            jax.ShapeDtypeStruct((s // ATT_K, D_SB, ATT_K), BF16),
            jax.ShapeDtypeStruct((s, D_RG), F32),
            jax.ShapeDtypeStruct((s, D_RG), F32),
        ],
        compiler_params=pltpu.CompilerParams(
            dimension_semantics=("arbitrary",), vmem_limit_bytes=VMEM_LIMIT_BYTES),
        name="inproj",
    )(x2d, gmix, wnat, wtq, wtv, qg, kg, bd)


def _attn_kernel(qt_ref, k_ref, vt_ref, tri_ref, ot_ref, d_ref, s_ref, w_ref):
    j = pl.program_id(1)
    tri = tri_ref[...]
    subs = ATT_Q // ATT_K
    q0 = j * ATT_Q
    for hh in range(2):
        _attn_head(hh, j, q0, subs, tri, qt_ref, k_ref, vt_ref, ot_ref, d_ref, s_ref, w_ref)


def _attn_head(hh, j, q0, subs, tri, qt_ref, k_ref, vt_ref, ot_ref, d_ref, s_ref, w_ref):
    rows = slice(hh * SB_HEAD_DIM, (hh + 1) * SB_HEAD_DIM)
    q = qt_ref[rows, :]
    zero = jnp.zeros_like(q)
    qpad = jnp.concatenate([q, zero] if hh == 0 else [zero, q], axis=0)

    def a_mm(sb):
        out = []
        for c in range(subs):
            ks = pl.multiple_of((sb * subs + c) * ATT_K, ATT_K)
            out.append(jnp.dot(k_ref[pl.ds(ks, ATT_K), :], qpad, preferred_element_type=F32))
        return out

    def a_ew(zs, sb, slot, masked):
        for c, z in enumerate(zs):
            sp = jnp.maximum(z, 0.0) + jnp.log(1.0 + jnp.exp(-jnp.abs(z)))
            d = z - sp
            if masked:
                kpos = (sb * subs + c) * ATT_K + lax.broadcasted_iota(jnp.int32, z.shape, 0)
                qpos = q0 + lax.broadcasted_iota(jnp.int32, z.shape, 1)
                valid = kpos < qpos
                sp = jnp.where(valid, sp, 0.0)
                d = jnp.where(valid, d, NEG_INF)
            d_ref[slot, c] = d
            s_ref[slot, c] = sp.astype(BF16)

    def b_mm(slot):
        return [jnp.dot(tri, s_ref[slot, c], preferred_element_type=F32) for c in range(subs)]

    def b_ew(sufs, slot, carry):
        for c in range(subs - 1, -1, -1):
            w_ref[slot, c] = jnp.exp(d_ref[slot, c] - sufs[c] - carry).astype(BF16)
            carry = carry + (sufs[c][0:1, :] + s_ref[slot, c, 0:1, :].astype(F32))
        return carry

    def c_mm(sb, slot, acc):
        for c in range(subs):
            acc = acc + jnp.dot(vt_ref[sb * subs + c, rows, :], w_ref[slot, c],
                                preferred_element_type=F32)
        return acc

    def tick(n, slot, carry, acc, do_a=True, do_b=True, do_c=True):
        if do_a:
            zs = a_mm(j - n)
        if do_b:
            sufs = b_mm(1 - slot)
        if do_c:
            acc = c_mm(j - n + 2, slot, acc)
        if do_a:
            a_ew(zs, j - n, slot, False)
        if do_b:
            carry = b_ew(sufs, 1 - slot, carry)
        return carry, acc

    n_steps = j + 1
    carry = jnp.zeros((1, ATT_Q), F32)
    acc = jnp.zeros((SB_HEAD_DIM, ATT_Q), F32)
    a_ew(a_mm(j), j, 0, True)

    def pipeline(start, ca):
        def single(ca):
            carry, acc = tick(start + 1, 1, *ca, do_a=False, do_c=False)
            return tick(start + 2, 0, carry, acc, do_a=False, do_b=False)

        def multi(ca):
            carry, acc = tick(start + 1, 1, *ca, do_c=False)
            n_pairs = (n_steps - start - 2) // 2

            def cond(state):
                p, done, _, _ = state
                return jnp.logical_and(p < n_pairs, done == 0)

            def body(state):
                p, _, carry, acc = state
                carry, acc = tick(start + 2 * p + 2, 0, carry, acc)
                done = (jnp.min(carry) >= ATT_ZERO_WEIGHT).astype(jnp.int32)
                carry, acc = tick(start + 2 * p + 3, 1, carry, acc)
                return p + 1, done, carry, acc

            _, done, carry, acc = lax.while_loop(cond, body, (jnp.int32(0), jnp.int32(0), carry, acc))

            def tail_even(ca):
                carry, acc = tick(n_steps, 0, *ca, do_a=False)
                return tick(n_steps + 1, 1, carry, acc, do_a=False, do_b=False)

            def tail_odd(ca):
                carry, acc = tick(n_steps - 1, 0, *ca)
                carry, acc = tick(n_steps, 1, carry, acc, do_a=False)
                return tick(n_steps + 1, 0, carry, acc, do_a=False, do_b=False)

            def tails(ca):
                return lax.cond(n_steps % 2 == 0, tail_even, tail_odd, ca)

            return lax.cond(done == 1, lambda ca: ca, tails, (carry, acc))

        return lax.cond(n_steps - start == 1, single, multi, ca)

    def two_steps_first(ca):
        carry, acc = ca
        a_ew(a_mm(j - 1), j - 1, 1, False)
        carry = b_ew(b_mm(0), 0, carry)
        carry = b_ew(b_mm(1), 1, carry)
        acc = c_mm(j - 1, 1, c_mm(j, 0, acc))
        finished = jnp.logical_or(jnp.min(carry) >= ATT_ZERO_WEIGHT, n_steps == 2)

        def rest(ca):
            a_ew(a_mm(j - 2), j - 2, 0, False)
            return pipeline(2, ca)

        return lax.cond(finished, lambda ca: ca, rest, (carry, acc))

    def only_diagonal(ca):
        carry, acc = tick(1, 1, *ca, do_a=False, do_c=False)
        return tick(2, 0, carry, acc, do_a=False, do_b=False)

    carry, acc = lax.cond(n_steps == 1, only_diagonal, two_steps_first, (carry, acc))
    ot_ref[rows, :] = acc


def _attention(qt, k, vt, tri):
    s = k.shape[0]
    return pl.pallas_call(
        _attn_kernel,
        grid=(SB_HEADS // 2, s // ATT_Q),
        in_specs=[
            pl.BlockSpec((2 * SB_HEAD_DIM, ATT_Q), lambda h, j: (h, j)),
            pl.BlockSpec((s, 2 * SB_HEAD_DIM), lambda h, j: (0, h)),
            pl.BlockSpec((s // ATT_K, 2 * SB_HEAD_DIM, ATT_K), lambda h, j: (0, h, 0)),
            pl.BlockSpec((ATT_K, ATT_K), lambda h, j: (0, 0)),
        ],
        out_specs=pl.BlockSpec((2 * SB_HEAD_DIM, ATT_Q), lambda h, j: (h, j)),
        out_shape=jax.ShapeDtypeStruct((D_SB, s), F32),
        scratch_shapes=[
            pltpu.VMEM((2, ATT_Q // ATT_K, ATT_K, ATT_Q), F32),
            pltpu.VMEM((2, ATT_Q // ATT_K, ATT_K, ATT_Q), BF16),
            pltpu.VMEM((2, ATT_Q // ATT_K, ATT_K, ATT_Q), BF16),
        ],
        compiler_params=pltpu.CompilerParams(
            dimension_semantics=("arbitrary", "arbitrary"), vmem_limit_bytes=VMEM_LIMIT_BYTES),
        name="sb_attention",
    )(qt, k, vt, tri)


def _mix_kernel(xrg_ref, grg_ref, ot_ref, x_ref, cw_ref, cb_ref, wa_ref, ba_ref, wx_ref, bx_ref,
                lam_ref, nsb_ref, nrg_ref, wout_ref, nffn_ref,
                x2_ref, h2t_ref, xs_ref, hs_ref, hstate_ref):
    t = xrg_ref.shape[0]
    hist = SUBLANES

    @pl.when(pl.program_id(0) == 0)
    def _():
        xs_ref[0:hist, :] = jnp.zeros((hist, D_RG), F32)
        hstate_ref[...] = jnp.zeros_like(hstate_ref)

    xs_ref[hist:hist + t, :] = xrg_ref[...]
    y = cb_ref[...] + cw_ref[0:1, :] * xs_ref[hist - 3:hist - 3 + t, :]
    for jj in range(1, CONV_WIDTH):
        y = y + cw_ref[jj:jj + 1, :] * xs_ref[hist - 3 + jj:hist - 3 + jj + t, :]
    xs_ref[0:hist, :] = xs_ref[t:t + hist, :]

    yb = y.astype(BF16)
    r = jax.nn.sigmoid(jnp.dot(yb, wa_ref[...], preferred_element_type=F32) + ba_ref[...])
    ig = jax.nn.sigmoid(jnp.dot(yb, wx_ref[...], preferred_element_type=F32) + bx_ref[...])
    log_a = (-RG_C) * r * _softplus(-lam_ref[...])
    a = jnp.exp(log_a)
    b = jnp.sqrt(jnp.tanh(-log_a) * (1.0 + a * a)) * (ig * y)

    rowmod = lax.broadcasted_iota(jnp.int32, (t, D_RG), 0) % SUBLANES
    d = 1
    while d < SUBLANES:
        keep = rowmod >= d
        a_sh = pltpu.roll(a, d, axis=0)
        b_sh = pltpu.roll(b, d, axis=0)
        b = jnp.where(keep, b + a * b_sh, b)
        a = jnp.where(keep, a * a_sh, a)
        d *= 2
    hprev = hstate_ref[...]
    for g in range(t // SUBLANES):
        sl = slice(g * SUBLANES, (g + 1) * SUBLANES)
        hg = a[sl] * hprev + b[sl]
        hs_ref[sl, :] = hg
        hprev = jnp.broadcast_to(hg[SUBLANES - 1:SUBLANES, :], (SUBLANES, D_RG))
    hstate_ref[...] = hprev

    o_rg = hs_ref[...] * _gelu_tanh(grg_ref[...])
    o_sb = ot_ref[...].T

    def rms(v, g):
        return v * lax.rsqrt(jnp.mean(v * v, axis=-1, keepdims=True) + EPS) * g

    n_sb = rms(o_sb, nsb_ref[...]).astype(BF16)
    n_rg = rms(o_rg, nrg_ref[...]).astype(BF16)
    mix = (jnp.dot(n_sb, wout_ref[0:D_SB, :], preferred_element_type=F32)
           + jnp.dot(n_rg, wout_ref[D_SB:, :], preferred_element_type=F32))
    x2 = x_ref[...] + mix
    x2_ref[...] = x2
    h2t_ref[...] = rms(x2, nffn_ref[...]).T.astype(BF16)


def _mix(xrg, grg, ot, x2d, cw, cb, wa, ba, wx, bx, lam, nsb, nrg, wout, nffn):
    s = x2d.shape[0]
    t = MIX_TOKENS
    const = lambda shape: pl.BlockSpec(shape, lambda i: (0,) * len(shape))
    return pl.pallas_call(
        _mix_kernel,
        grid=(s // t,),
        in_specs=[
            pl.BlockSpec((t, D_RG), lambda i: (i, 0)),
            pl.BlockSpec((t, D_RG), lambda i: (i, 0)),
            pl.BlockSpec((D_SB, t), lambda i: (0, i)),
            pl.BlockSpec((t, D_MODEL), lambda i: (i, 0)),
            const((CONV_WIDTH, D_RG)), const((1, D_RG)),
            const((D_RG, D_RG)), const((1, D_RG)),
            const((D_RG, D_RG)), const((1, D_RG)),
            const((1, D_RG)), const((1, D_SB)), const((1, D_RG)),
            const((D_MODEL, D_MODEL)), const((1, D_MODEL)),
        ],
        out_specs=[
            pl.BlockSpec((t, D_MODEL), lambda i: (i, 0)),
            pl.BlockSpec((D_MODEL, t), lambda i: (0, i)),
        ],
        out_shape=[
            jax.ShapeDtypeStruct((s, D_MODEL), F32),
            jax.ShapeDtypeStruct((D_MODEL, s), BF16),
        ],
        scratch_shapes=[
            pltpu.VMEM((t + SUBLANES, D_RG), F32),
            pltpu.VMEM((t, D_RG), F32),
            pltpu.VMEM((SUBLANES, D_RG), F32),
        ],
        compiler_params=pltpu.CompilerParams(
            dimension_semantics=("arbitrary",), vmem_limit_bytes=VMEM_LIMIT_BYTES),
        name="rglru_outproj",
    )(xrg, grg, ot, x2d, cw, cb, wa, ba, wx, bx, lam, nsb, nrg, wout, nffn)


_N_RANK = PEER_TOPK + 1
_CAND_PAIRS = [(a, b) for a in range(_N_RANK) for b in range(_N_RANK)
               if (a + 1) * (b + 1) <= _N_RANK]
_CAND_ROWS = 64


def _sort_network(n):
    pairs = []
    p = 1
    while p < n:
        k = p
        while k >= 1:
            for j in range(k % p, n - k, 2 * k):
                for i in range(min(k, n - j - k)):
                    if (i + j) // (2 * p) == (i + j + k) // (2 * p):
                        pairs.append((i + j, i + j + k))
            k //= 2
        p *= 2
    return pairs


def _top_desc(s, n):
    out = []
    for r in range(n):
        m = jnp.max(s, axis=0, keepdims=True)
        out.append(m)
        if r + 1 < n:
            s = jnp.where(s == m, NEG_INF, s)
    return out


def _top_desc_sorted(s, n):
    groups = s.shape[0] // SUBLANES
    cols = [s[g * SUBLANES:(g + 1) * SUBLANES] for g in range(groups)]
    for a, b in _sort_network(groups):
        cols[a], cols[b] = jnp.maximum(cols[a], cols[b]), jnp.minimum(cols[a], cols[b])
    out = []
    for r in range(n):
        m = jnp.max(cols[0], axis=0, keepdims=True)
        out.append(m)
        remaining = n - 1 - r
        if remaining:
            hit = cols[0] == m
            for d in range(min(groups, remaining)):
                nxt = cols[d + 1] if d + 1 < groups else NEG_INF
                cols[d] = jnp.where(hit, nxt, cols[d])
    return out


def _peer_route(ht, wq_ref, keys_ref, e1_ref, th_ref, e2_ref, s2_ref):
    for hd in range(PEER_HEADS):
        qt = jnp.dot(wq_ref[hd * PEER_D_KEY:(hd + 1) * PEER_D_KEY, :], ht,
                     preferred_element_type=F32).astype(BF16)
        s1 = jnp.dot(keys_ref[hd, 0], qt[:PEER_HALF], preferred_element_type=F32)
        s2 = jnp.dot(keys_ref[hd, 1], qt[PEER_HALF:], preferred_element_type=F32)
        top1 = _top_desc_sorted(s1, _N_RANK)
        top2 = _top_desc_sorted(s2, _N_RANK)
        cand = [top1[a] + top2[b] for a, b in _CAND_PAIRS]
        cand += [jnp.full_like(cand[0], NEG_INF)] * (_CAND_ROWS - len(cand))
        best = _top_desc_sorted(jnp.concatenate(cand, axis=0), _N_RANK)
        zsum = jnp.ones_like(best[0])
        for kk in range(1, PEER_TOPK):
            zsum = zsum + jnp.exp(best[kk] - best[0])
        tau = 0.5 * (best[PEER_TOPK - 1] + best[PEER_TOPK])
        e1_ref[hd] = jnp.exp(s1 - top1[0]) / zsum
        th_ref[hd] = tau - s1
        e2_ref[hd] = jnp.exp(s2 - top2[0])
        s2_ref[hd] = s2


def _peer_kernel(ht_ref, x2_ref, wq_ref, keys_ref, u_ref, vt_ref, out_ref,
                 e1_ref, th_ref, e2_ref, s2_ref, p_ref, acc_ref):
    e = pl.program_id(1)
    t = ht_ref.shape[1]
    n1 = PEER_EXPERTS // PEER_N_KEYS
    assert n1 == SUBLANES

    @pl.when(e == 0)
    def _():
        _peer_route(ht_ref[...], wq_ref, keys_ref, e1_ref, th_ref, e2_ref, s2_ref)
        acc_ref[...] = jnp.zeros_like(acc_ref)

    i1_base = pl.multiple_of(e * n1, SUBLANES)
    il_per_chunk = PEER_CHUNK // PEER_N_KEYS
    half = PEER_N_KEYS // 2
    n_chunks = PEER_EXPERTS // PEER_CHUNK
    chunk = lambda ch: slice(ch * PEER_CHUNK, (ch + 1) * PEER_CHUNK)
    pre_all = jnp.dot(u_ref[...], ht_ref[...], preferred_element_type=F32)
    for ch in range(n_chunks):
        pre = pre_all[chunk(ch)]
        for tc in range(t // LANES):
            ls = slice(tc * LANES, (tc + 1) * LANES)
            th = [th_ref[hd, pl.ds(i1_base, n1), ls] for hd in range(PEER_HEADS)]
            e1 = [e1_ref[hd, pl.ds(i1_base, n1), ls] for hd in range(PEER_HEADS)]
            act = _gelu_tanh(pre[:, ls])
            for hf in range(2):
                ks = slice(hf * half, (hf + 1) * half)
                w = [jnp.zeros((half, LANES), F32) for _ in range(il_per_chunk)]
                for hd in range(PEER_HEADS):
                    s2 = s2_ref[hd, ks, ls]
                    e2 = e2_ref[hd, ks, ls]
                    for ii in range(il_per_chunk):
                        il = ch * il_per_chunk + ii
                        w[ii] = w[ii] + jnp.where(s2 > th[hd][il:il + 1], e1[hd][il:il + 1] * e2, 0.0)
                for ii in range(il_per_chunk):
                    r0 = ii * PEER_N_KEYS + hf * half
                    p_ref[ch * PEER_CHUNK + r0:ch * PEER_CHUNK + r0 + half, ls] = (
                        w[ii] * act[r0:r0 + half]).astype(BF16)
    acc_ref[...] += jnp.dot(vt_ref[...], p_ref[...], preferred_element_type=F32)

    @pl.when(e == pl.num_programs(1) - 1)
    def _():
        out_ref[...] = x2_ref[...] + acc_ref[...].T


def _peer(h2t, x2, wq_t, keys, u_bf, vt_bf):
    s = h2t.shape[1]
    t = PEER_TOKENS
    n_exp = u_bf.shape[0]
    et = PEER_EXPERTS
    rt = lambda: pltpu.VMEM((PEER_HEADS, PEER_N_KEYS, t), F32)
    return pl.pallas_call(
        _peer_kernel,
        grid=(s // t, n_exp // et),
        in_specs=[
            pl.BlockSpec((D_MODEL, t), lambda i, e: (0, i)),
            pl.BlockSpec((t, D_MODEL), lambda i, e: (i, 0)),
            pl.BlockSpec((PEER_HEADS * PEER_D_KEY, D_MODEL), lambda i, e: (0, 0)),
            pl.BlockSpec((PEER_HEADS, 2, PEER_N_KEYS, PEER_HALF), lambda i, e: (0, 0, 0, 0)),
            pl.BlockSpec((et, D_MODEL), lambda i, e: (e, 0)),
            pl.BlockSpec((D_MODEL, et), lambda i, e: (0, e)),
        ],
        out_specs=pl.BlockSpec((t, D_MODEL), lambda i, e: (i, 0)),
        out_shape=jax.ShapeDtypeStruct((s, D_MODEL), F32),
        scratch_shapes=[rt(), rt(), rt(), rt(),
                        pltpu.VMEM((et, t), BF16),
                        pltpu.VMEM((D_MODEL, t), F32)],
        compiler_params=pltpu.CompilerParams(
            dimension_semantics=("arbitrary", "arbitrary"), vmem_limit_bytes=VMEM_LIMIT_BYTES),
        name="peer",
    )(h2t, x2, wq_t, keys, u_bf, vt_bf)


def _block_diag(w):
    n, bi, bj = w.shape
    eye = jnp.eye(n, dtype=w.dtype)
    return (eye[:, None, :, None] * w[:, :, None, :]).reshape(n * bi, n * bj)


def kernel(x, norm_mix, w_in, q_norm, k_norm, conv_w, conv_b, rg_w_a, rg_b_a, rg_w_x, rg_b_x,
           rg_lambda, out_norm_sb, out_norm_rg, w_out, norm_ffn, peer_w_query, peer_sub_keys,
           peer_u, peer_v):
    bsz, s, d = x.shape
    assert bsz == 1 and d == D_MODEL
    assert s % max(INPROJ_TOKENS, ATT_Q, MIX_TOKENS, PEER_TOKENS) == 0
    depth = w_in.shape[0]
    x2d = x.reshape(s, d)

    head_id = jnp.arange(D_SB) // SB_HEAD_DIM
    bd = (head_id[:, None] == head_id[None, :]).astype(BF16)
    kidx = jnp.arange(ATT_K)
    tri = (kidx[None, :] > kidx[:, None]).astype(BF16)
    row = lambda v: v.reshape(1, -1)

    for l in range(depth):
        w = w_in[l]
        wtq = w[:, 0:D_SB].T.astype(BF16)
        wtv = w[:, 2 * D_SB:3 * D_SB].T.astype(BF16)
        wnat = jnp.concatenate([w[:, D_SB:2 * D_SB], w[:, 3 * D_SB:]], axis=1).astype(BF16)
        qg = jnp.tile(q_norm[l], SB_HEADS).reshape(D_SB, 1)
        kg = jnp.tile(k_norm[l], SB_HEADS).reshape(1, D_SB)
        qt, k, vt, xrg, grg = _inproj(x2d, row(norm_mix[l]), wnat, wtq, wtv, qg, kg, bd)

        ot = _attention(qt, k, vt, tri)

        x2, h2t = _mix(xrg, grg, ot, x2d, conv_w[l], row(conv_b[l]),
                       _block_diag(rg_w_a[l]).astype(BF16), row(rg_b_a[l]),
                       _block_diag(rg_w_x[l]).astype(BF16), row(rg_b_x[l]),
                       row(rg_lambda[l]), row(out_norm_sb[l]), row(out_norm_rg[l]),
                       w_out[l].astype(BF16), row(norm_ffn[l]))

        wq_t = peer_w_query[l].reshape(d, PEER_HEADS * PEER_D_KEY).T.astype(BF16)
        x2d = _peer(h2t, x2, wq_t, peer_sub_keys[l].astype(BF16),
                    peer_u[l].astype(BF16), peer_v[l].T.astype(BF16))
    return x2d.reshape(bsz, s, d)
```
